```python
import math, functools
import jax, jax.numpy as jnp
from jax import lax
import numpy as np

D_MODEL = 1024
BATCH = 1
SEQ = 16384
DEPTH = 4

N_MIXERS = 3
CHUNK = 128
NORM_EPS = 1e-6
OUT_SCALE = 0.5

SSD_EXPAND = 2
SSD_DI = SSD_EXPAND * D_MODEL
SSD_HEADDIM = 64
SSD_HEADS = SSD_DI // SSD_HEADDIM
SSD_GROUPS = 8
SSD_STATE = 128
SSD_CONV = 4
SSD_CONV_DIM = SSD_DI + 2 * SSD_GROUPS * SSD_STATE
SSD_IN = SSD_DI + SSD_CONV_DIM + SSD_HEADS
DT_MIN = 0.001
DT_MAX = 0.1

RET_HEADS = 4
RET_DK = D_MODEL // RET_HEADS
RET_DV = 2 * RET_DK
RET_DVTOT = RET_HEADS * RET_DV
RET_THETA = 10000.0
RET_IN = 2 * RET_HEADS * RET_DK + 2 * RET_DVTOT

DSA_HEADS = 16
DSA_HEADDIM = 64
DSA_KV_HEADS = 4
DSA_GQA = DSA_HEADS // DSA_KV_HEADS
DSA_WIDTH = DSA_HEADS * DSA_HEADDIM
IDX_HEADS = 16
IDX_DIM = 64
TOPK_MAX = 256
Q_BLOCK = 128
ROPE_THETA = 500000.0
ROPE_FRACTION = 4
DSA_SPLITS = (DSA_WIDTH, DSA_KV_HEADS * DSA_HEADDIM, DSA_KV_HEADS * DSA_HEADDIM,
              DSA_WIDTH, IDX_HEADS * IDX_DIM, IDX_DIM, IDX_HEADS)
DSA_IN = sum(DSA_SPLITS)

kernel_name = "hybrid_ssd_retention_dsa_trunk"


def _rmsnorm(x, g):
    xf = x.astype(jnp.float32)
    y = xf * lax.rsqrt(jnp.mean(xf * xf, axis=-1, keepdims=True) + NORM_EPS)
    return (y * g.astype(jnp.float32)).astype(x.dtype)


def _rope(x, pos, inv_freq):
    half = inv_freq.shape[0]
    r = 2 * half
    ang = pos.astype(jnp.float32)[..., None] * inv_freq
    cos = jnp.cos(ang)[:, :, None, :]
    sin = jnp.sin(ang)[:, :, None, :]
    x1 = x[..., :half].astype(jnp.float32)
    x2 = x[..., half:r].astype(jnp.float32)
    rot = jnp.concatenate([x1 * cos - x2 * sin, x2 * cos + x1 * sin], axis=-1).astype(x.dtype)
    return jnp.concatenate([rot, x[..., r:]], axis=-1)


def _partial_inv_freq(head_dim):
    r = head_dim // ROPE_FRACTION
    return jnp.float32(ROPE_THETA) ** (-jnp.arange(0, r, 2, dtype=jnp.float32) / r)


def _split(t, sizes):
    idx = list(np.cumsum(sizes)[:-1])
    return jnp.split(t, idx, axis=-1)


def _causal_dwconv(x, w, b):
    c = x.shape[-1]
    y = lax.conv_general_dilated(x, w[:, None, :].astype(x.dtype), window_strides=(1,),
                                 padding=[(SSD_CONV - 1, 0)],
                                 dimension_numbers=("NWC", "WIO", "NWC"),
                                 feature_group_count=c)
    return y + b


def _ssd_chunked(xs, dt, bm, cm, a_log):
    f32 = jnp.float32
    b, L = xs.shape[:2]
    nc = L // CHUNK
    hpg = SSD_HEADS // SSD_GROUPS
    A = -jnp.exp(a_log.astype(f32))
    X = (xs * dt[..., None]).reshape(b, nc, CHUNK, SSD_GROUPS, hpg, SSD_HEADDIM)
    a = (dt * A).reshape(b, nc, CHUNK, SSD_GROUPS, hpg)
    bc = bm.reshape(b, nc, CHUNK, SSD_GROUPS, SSD_STATE)
    cc = cm.reshape(b, nc, CHUNK, SSD_GROUPS, SSD_STATE)
    a_cum = jnp.cumsum(a, axis=2)
    a_t = jnp.moveaxis(a_cum, 2, -1)
    causal = jnp.tril(jnp.ones((CHUNK, CHUNK), dtype=bool))
    decay = jnp.exp(jnp.where(causal, a_t[..., :, None] - a_t[..., None, :], -jnp.inf))
    cb = jnp.einsum("bclgn,bcsgn->bcgls", cc, bc)
    y_diag = jnp.einsum("bcgels,bcsgep->bclgep", cb[:, :, :, None] * decay, X)
    to_end = jnp.exp(a_cum[:, :, -1:] - a_cum)
    states = jnp.einsum("bclgn,bclgep->bcgepn", bc, X * to_end[..., None])
    chunk_decay = jnp.exp(a_cum[:, :, -1])

    def step(s, inp):
        st, dec = inp
        return dec[..., None, None] * s + st, s

    s0 = jnp.zeros((b, SSD_GROUPS, hpg, SSD_HEADDIM, SSD_STATE), f32)
    _, prev = lax.scan(step, s0, (jnp.moveaxis(states, 1, 0), jnp.moveaxis(chunk_decay, 1, 0)))
    y_off = jnp.einsum("bclgn,cbgepn->bclgep", cc, prev) * jnp.exp(a_cum)[..., None]
    return (y_diag + y_off).reshape(b, L, SSD_HEADS, SSD_HEADDIM)


def _ssd_mixer(h, w_in, conv_w, conv_b, dt_bias, a_log, d_skip, gnorm, w_out):
    f32 = jnp.float32
    b, L, _ = h.shape
    z, xbc, dt = _split(h @ w_in, (SSD_DI, SSD_CONV_DIM, SSD_HEADS))
    xbc = jax.nn.silu(_causal_dwconv(xbc, conv_w, conv_b))
    xs, bm, cm = _split(xbc.astype(f32), (SSD_DI, SSD_GROUPS * SSD_STATE, SSD_GROUPS * SSD_STATE))
    xs = xs.reshape(b, L, SSD_HEADS, SSD_HEADDIM)
    dt = jax.nn.softplus(dt.astype(f32) + dt_bias.astype(f32))
    y = _ssd_chunked(xs, dt, bm.reshape(b, L, SSD_GROUPS, SSD_STATE),
                     cm.reshape(b, L, SSD_GROUPS, SSD_STATE), a_log)
    y = y + d_skip.astype(f32)[:, None] * xs
    y = y.reshape(b, L, SSD_DI) * jax.nn.silu(z.astype(f32))
    y = _rmsnorm(y.reshape(b, L, SSD_GROUPS, SSD_DI // SSD_GROUPS),
                 gnorm.reshape(SSD_GROUPS, SSD_DI // SSD_GROUPS)).reshape(b, L, SSD_DI)
    return y.astype(h.dtype) @ w_out


def _retention_chunked(q, k, v):
    f32 = jnp.float32
    b, L = q.shape[:2]
    nc = L // CHUNK
    log_g = jnp.log1p(-jnp.exp2(-5.0 - jnp.arange(RET_HEADS, dtype=f32)))
    qc = q.astype(f32).reshape(b, nc, CHUNK, RET_HEADS, RET_DK)
    kc = k.astype(f32).reshape(b, nc, CHUNK, RET_HEADS, RET_DK)
    vc = v.astype(f32).reshape(b, nc, CHUNK, RET_HEADS, RET_DV)
    i = jnp.arange(CHUNK, dtype=f32)
    diff = i[:, None] - i[None, :]
    intra = jnp.where(diff >= 0, jnp.exp(jnp.maximum(diff, 0.0)[None] * log_g[:, None, None]), 0.0)
    scores = jnp.einsum("bclhd,bcshd->bchls", qc, kc) * intra
    o_intra = jnp.einsum("bchls,bcshe->bclhe", scores, vc)
    k_dec = kc * jnp.exp((CHUNK - 1 - i)[:, None] * log_g)[:, :, None]
    chunk_kv = jnp.einsum("bclhd,bclhe->bchde", k_dec, vc)
    chunk_decay = jnp.exp(CHUNK * log_g)

    def step(s, kv):
        return chunk_decay[:, None, None] * s + kv, s

    s0 = jnp.zeros((b, RET_HEADS, RET_DK, RET_DV), f32)
    _, prev = lax.scan(step, s0, jnp.moveaxis(chunk_kv, 1, 0))
    q_dec = qc * jnp.exp((i + 1.0)[:, None] * log_g)[:, :, None]
    o_cross = jnp.einsum("bclhd,cbhde->bclhe", q_dec, prev)
    return (o_intra + o_cross).reshape(b, L, RET_HEADS, RET_DV)


def _retention_mixer(h, positions, w_in, gnorm, w_out):
    f32 = jnp.float32
    b, L, _ = h.shape
    q, k, v, g = _split(h @ w_in, (RET_HEADS * RET_DK, RET_HEADS * RET_DK, RET_DVTOT, RET_DVTOT))
    inv = 1.0 / (jnp.float32(RET_THETA) ** jnp.linspace(0.0, 1.0, RET_DK // 2, dtype=f32))
    q = _rope(q.reshape(b, L, RET_HEADS, RET_DK), positions, inv)
    k = _rope(k.reshape(b, L, RET_HEADS, RET_DK), positions, inv) * (RET_DK ** -0.5)
    o = _retention_chunked(q, k, v.reshape(b, L, RET_HEADS, RET_DV))
    o = _rmsnorm(o, gnorm.reshape(RET_HEADS, RET_DV)).reshape(b, L, RET_DVTOT)
    o = o * jax.nn.silu(g.astype(f32))
    return o.astype(h.dtype) @ w_out


def _dsa_sparse_attention(q, k, v, qi, ki, wi, topk):
    f32 = jnp.float32
    L = q.shape[0]
    nb = L // Q_BLOCK
    key_pos = jnp.arange(L)
    kif = ki.astype(f32)

    def block(args):
        qb, qib, wb, j = args
        q_pos = j * Q_BLOCK + jnp.arange(Q_BLOCK)
        rel = jax.nn.relu(jnp.einsum("thd,sd->ths", qib.astype(f32), kif) * (IDX_DIM ** -0.5))
        score = jnp.einsum("th,ths->ts", wb.astype(f32), rel)
        score = jnp.where(key_pos[None, :] <= q_pos[:, None], score, -jnp.inf)
        _, sel = lax.top_k(score, topk)
        valid = sel <= q_pos[:, None]
        ks = k[sel].astype(f32)
        vs = v[sel].astype(f32)
        qg = qb.reshape(Q_BLOCK, DSA_KV_HEADS, DSA_GQA, DSA_HEADDIM).astype(f32)
        logits = jnp.einsum("tngd,tknd->tngk", qg, ks) * (DSA_HEADDIM ** -0.5)
        logits = jnp.where(valid[:, None, None, :], logits, -jnp.inf)
        p = jax.nn.softmax(logits, axis=-1)
        ob = jnp.einsum("tngk,tknd->tngd", p, vs)
        return ob.reshape(Q_BLOCK, DSA_HEADS, DSA_HEADDIM).astype(q.dtype)

    blocks = (q.reshape(nb, Q_BLOCK, DSA_HEADS, DSA_HEADDIM),
              qi.reshape(nb, Q_BLOCK, IDX_HEADS, IDX_DIM),
              wi.reshape(nb, Q_BLOCK, IDX_HEADS),
              jnp.arange(nb))
    return lax.map(block, blocks).reshape(L, DSA_HEADS, DSA_HEADDIM)


def _dsa_mixer(h, positions, w_in, idx_knorm, w_out):
    f32 = jnp.float32
    b, L, _ = h.shape
    q, k, v, g, qi, ki, wi = _split(h @ w_in, DSA_SPLITS)
    inv_a = _partial_inv_freq(DSA_HEADDIM)
    inv_i = _partial_inv_freq(IDX_DIM)
    q = _rope(q.reshape(b, L, DSA_HEADS, DSA_HEADDIM), positions, inv_a)
    k = _rope(k.reshape(b, L, DSA_KV_HEADS, DSA_HEADDIM), positions, inv_a)
    v = v.reshape(b, L, DSA_KV_HEADS, DSA_HEADDIM)
    qi = _rope(qi.reshape(b, L, IDX_HEADS, IDX_DIM), positions, inv_i)
    ki = _rope(_rmsnorm(ki, idx_knorm)[:, :, None, :], positions, inv_i)[:, :, 0, :]
    wi = wi * (IDX_HEADS ** -0.5)
    topk = min(TOPK_MAX, L // 4)
    o = jax.vmap(functools.partial(_dsa_sparse_attention, topk=topk))(q, k, v, qi, ki, wi)
    o = o.reshape(b, L, DSA_WIDTH).astype(f32) * jax.nn.silu(g.astype(f32))
    return o.astype(h.dtype) @ w_out


def _gain(key, n):
    return 1.0 + 0.02 * jax.random.normal(key, (n,), jnp.float32)


def _ssd_params(key, p):
    k = jax.random.split(key, 9)
    dt = jnp.exp(jax.random.uniform(k[3], (SSD_HEADS,), jnp.float32,
                                    minval=math.log(DT_MIN), maxval=math.log(DT_MAX)))
    return {
        p + "norm": _gain(k[0], D_MODEL),
        p + "w_in": jax.random.normal(k[1], (D_MODEL, SSD_IN), jnp.float32) * D_MODEL ** -0.5,
        p + "conv_w": jax.random.normal(k[2], (SSD_CONV, SSD_CONV_DIM), jnp.float32) * SSD_CONV ** -0.5,
        p + "conv_b": 0.02 * jax.random.normal(k[4], (SSD_CONV_DIM,), jnp.float32),
        p + "dt_bias": dt + jnp.log(-jnp.expm1(-dt)),
        p + "a_log": jnp.log(jax.random.uniform(k[5], (SSD_HEADS,), jnp.float32, minval=1.0, maxval=16.0)),
        p + "d_skip": _gain(k[6], SSD_HEADS),
        p + "gnorm": _gain(k[7], SSD_DI),
        p + "w_out": jax.random.normal(k[8], (SSD_DI, D_MODEL), jnp.float32) * SSD_DI ** -0.5 * OUT_SCALE,
    }


def _ret_params(key, p):
    k = jax.random.split(key, 4)
    return {
        p + "norm": _gain(k[0], D_MODEL),
        p + "w_in": jax.random.normal(k[1], (D_MODEL, RET_IN), jnp.float32) * D_MODEL ** -0.5,
        p + "gnorm": _gain(k[2], RET_DVTOT),
        p + "w_out": jax.random.normal(k[3], (RET_DVTOT, D_MODEL), jnp.float32) * RET_DVTOT ** -0.5 * OUT_SCALE,
    }


def _dsa_params(key, p):
    k = jax.random.split(key, 4)
    return {
        p + "norm": _gain(k[0], D_MODEL),
        p + "w_in": jax.random.normal(k[1], (D_MODEL, DSA_IN), jnp.float32) * D_MODEL ** -0.5,
        p + "idx_knorm": _gain(k[2], IDX_DIM),
        p + "w_out": jax.random.normal(k[3], (DSA_WIDTH, D_MODEL), jnp.float32) * DSA_WIDTH ** -0.5 * OUT_SCALE,
    }


def setup_inputs(seed: int = 0) -> dict:
    key = jax.random.key(seed)
    kx, k0, k1, k2, k3, kf = jax.random.split(key, 6)
    inputs = {
        "x": jax.random.normal(kx, (BATCH, SEQ, D_MODEL), jnp.float32),
        "positions": jnp.broadcast_to(jnp.arange(SEQ, dtype=jnp.int32), (BATCH, SEQ)),
    }
    inputs.update(_ssd_params(k0, "l0_"))
    inputs.update(_ret_params(k1, "l1_"))
    inputs.update(_dsa_params(k2, "l2_"))
    inputs.update(_ssd_params(k3, "l3_"))
    inputs["final_norm"] = _gain(kf, D_MODEL)
    return inputs


def reference(x, positions,
              l0_norm, l0_w_in, l0_conv_w, l0_conv_b, l0_dt_bias, l0_a_log, l0_d_skip, l0_gnorm, l0_w_out,
              l1_norm, l1_w_in, l1_gnorm, l1_w_out,
              l2_norm, l2_w_in, l2_idx_knorm, l2_w_out,
              l3_norm, l3_w_in, l3_conv_w, l3_conv_b, l3_dt_bias, l3_a_log, l3_d_skip, l3_gnorm, l3_w_out,
              final_norm):
    layer_params = [
        (l0_norm, (l0_w_in, l0_conv_w, l0_conv_b, l0_dt_bias, l0_a_log, l0_d_skip, l0_gnorm, l0_w_out)),
        (l1_norm, (l1_w_in, l1_gnorm, l1_w_out)),
        (l2_norm, (l2_w_in, l2_idx_knorm, l2_w_out)),
        (l3_norm, (l3_w_in, l3_conv_w, l3_conv_b, l3_dt_bias, l3_a_log, l3_d_skip, l3_gnorm, l3_w_out)),
    ]
    h = x
    for i in range(DEPTH):
        norm_g, p = layer_params[i]
        hn = _rmsnorm(h, norm_g)
        kind = i % N_MIXERS
        if kind == 0:
            y = _ssd_mixer(hn, *p)
        elif kind == 1:
            y = _retention_mixer(hn, positions, *p)
        else:
            y = _dsa_mixer(hn, positions, *p)
        h = h + y
    return _rmsnorm(h, final_norm)
```

```python
import functools
import math

import numpy as np
import jax
import jax.numpy as jnp
from jax import lax
from jax.experimental import pallas as pl
from jax.experimental.pallas import tpu as pltpu

F32 = jnp.float32
BF16 = jnp.bfloat16
I32 = jnp.int32

LANES = 128
SUBLANES = 8
VMEM_LIMIT_BYTES = 56 * 1024 * 1024

D_MODEL = 1024
NORM_EPS = 1e-6
CHUNK = 128

SSD_DI = 2048
SSD_HEADDIM = 64
SSD_HEADS = 32
SSD_GROUPS = 8
SSD_STATE = 128
SSD_CONV = 4
SSD_CONV_DIM = 4096
SSD_IN = SSD_DI + SSD_CONV_DIM + SSD_HEADS
SSD_IN_PAD = SSD_DI + SSD_CONV_DIM + LANES

RET_HEADS = 4
RET_DK = 256
RET_DV = 512
RET_DVTOT = 2048
RET_THETA = 10000.0
RET_IN = 2 * RET_HEADS * RET_DK + 2 * RET_DVTOT

DSA_HEADS = 16
DSA_HEADDIM = 64
DSA_KV_HEADS = 4
DSA_GQA = 4
DSA_WIDTH = 1024
IDX_HEADS = 16
IDX_DIM = 64
TOPK_MAX = 256
ROPE_THETA = 500000.0
ROPE_FRACTION = 4
DSA_IN_PAD = 3840
DSA_QBLOCK = 128
DSA_KTILE = 256

INT_MIN = -(2 ** 31)
NEG_BIG = -1e30


def _cparams(sem, vmem=None):
    return pltpu.CompilerParams(dimension_semantics=sem, vmem_limit_bytes=vmem)


def _sigmoid(x):
    return 1.0 / (1.0 + jnp.exp(-x))


def _silu(x):
    return x * _sigmoid(x)


def _dot(a, b):
    return jnp.dot(a, b, preferred_element_type=F32)


def _dot_nt(a, b):
    return lax.dot_general(a, b, (((1,), (1,)), ((), ())), preferred_element_type=F32)


def _dot_tn(a, b):
    return lax.dot_general(a, b, (((0,), (0,)), ((), ())), preferred_element_type=F32)


def _norm_matmul_kernel(h_ref, g_ref, w_ref, o_ref, hn_ref):
    @pl.when(pl.program_id(1) == 0)
    def _():
        x = h_ref[...]
        ms = jnp.mean(x * x, axis=-1, keepdims=True)
        hn_ref[...] = (x * lax.rsqrt(ms + NORM_EPS) * g_ref[...]).astype(BF16)

    o_ref[...] = _dot(hn_ref[...], w_ref[...])


def _norm_matmul(h, g, w_bf16, tm, tn):
    L, d = h.shape
    n = w_bf16.shape[1]
    return pl.pallas_call(
        _norm_matmul_kernel,
        grid=(L // tm, n // tn),
        in_specs=[
            pl.BlockSpec((tm, d), lambda i, j: (i, 0)),
            pl.BlockSpec((1, d), lambda i, j: (0, 0)),
            pl.BlockSpec((d, tn), lambda i, j: (0, j)),
        ],
        out_specs=pl.BlockSpec((tm, tn), lambda i, j: (i, j)),
        out_shape=jax.ShapeDtypeStruct((L, n), F32),
        scratch_shapes=[pltpu.VMEM((tm, d), BF16)],
        compiler_params=_cparams(("parallel", "arbitrary"), VMEM_LIMIT_BYTES),
        name="norm_in_proj",
    )(h, g.reshape(1, d), w_bf16)


def _out_proj_kernel(y_ref, w_ref, h_ref, o_ref):
    o_ref[...] = h_ref[...] + _dot(y_ref[...], w_ref[...])


def _out_proj_final_kernel(y_ref, w_ref, h_ref, g_ref, o_ref):
    x = h_ref[...] + _dot(y_ref[...], w_ref[...])
    ms = jnp.mean(x * x, axis=-1, keepdims=True)
    o_ref[...] = x * lax.rsqrt(ms + NORM_EPS) * g_ref[...]


def _out_proj(y_bf16, w_bf16, h, final_g=None, tm=512):
    L, k = y_bf16.shape
    d = w_bf16.shape[1]
    in_specs = [
        pl.BlockSpec((tm, k), lambda i: (i, 0)),
        pl.BlockSpec((k, d), lambda i: (0, 0)),
        pl.BlockSpec((tm, d), lambda i: (i, 0)),
    ]
    args = [y_bf16, w_bf16, h]
    body = _out_proj_kernel
    if final_g is not None:
        in_specs.append(pl.BlockSpec((1, d), lambda i: (0, 0)))
        args.append(final_g.reshape(1, d))
        body = _out_proj_final_kernel
    return pl.pallas_call(
        body,
        grid=(L // tm,),
        in_specs=in_specs,
        out_specs=pl.BlockSpec((tm, d), lambda i: (i, 0)),
        out_shape=jax.ShapeDtypeStruct((L, d), F32),
        compiler_params=_cparams(("parallel",), VMEM_LIMIT_BYTES),
        name="out_proj_residual",
    )(*args)


def _rope_table_kernel(pos_ref, inv_ret_ref, inv_dsa_ref, rc_ref, rs_ref, dc_ref, ds1_ref, ds2_ref):
    pos = pos_ref[...]
    ang = pos * inv_ret_ref[...]
    rc_ref[...] = jnp.cos(ang)
    rs_ref[...] = jnp.sin(ang)
    ang_d = pos * inv_dsa_ref[...]
    lane = lax.broadcasted_iota(I32, ang_d.shape, 1) % DSA_HEADDIM
    half = DSA_HEADDIM // ROPE_FRACTION // 2
    c = jnp.cos(ang_d)
    s = jnp.sin(ang_d)
    dc_ref[...] = jnp.where(lane < 2 * half, c, 1.0)
    ds1_ref[...] = jnp.where(lane < half, -s, 0.0)
    ds2_ref[...] = jnp.where((lane >= half) & (lane < 2 * half), s, 0.0)


def _rope_tables(positions_f32, tl=1024):
    L = positions_f32.shape[0]
    inv_ret = 1.0 / (jnp.float32(RET_THETA) ** jnp.linspace(0.0, 1.0, RET_DK // 2, dtype=F32))
    r = DSA_HEADDIM // ROPE_FRACTION
    inv_dsa = jnp.float32(ROPE_THETA) ** (-jnp.arange(0, r, 2, dtype=F32) / r)
    inv_dsa_lanes = jnp.tile(inv_dsa, LANES // inv_dsa.shape[0])
    tab = jax.ShapeDtypeStruct((L, LANES), F32)
    spec = pl.BlockSpec((tl, LANES), lambda i: (i, 0))
    cst = pl.BlockSpec((1, LANES), lambda i: (0, 0))
    return pl.pallas_call(
        _rope_table_kernel,
        grid=(L // tl,),
        in_specs=[pl.BlockSpec((tl, 1), lambda i: (i, 0)), cst, cst],
        out_specs=[spec] * 5,
        out_shape=[tab] * 5,
        compiler_params=_cparams(("parallel",)),
        name="rope_tables",
    )(positions_f32.reshape(L, 1), inv_ret.reshape(1, LANES), inv_dsa_lanes.reshape(1, LANES))


def _pair_bcast(m, j):
    rows = m.shape[0]
    lane = lax.broadcasted_iota(I32, (rows, LANES), 1)
    lo = jnp.broadcast_to(m[:, 2 * j:2 * j + 1], (rows, LANES))
    hi = jnp.broadcast_to(m[:, 2 * j + 1:2 * j + 2], (rows, LANES))
    return jnp.where(lane < SSD_HEADDIM, lo, hi)


def _ssd_kernel(z_ref, x_ref, b_ref, c_ref, dt_ref, cw_ref, cb_ref, dtb_ref, aneg_ref, dskip_ref, gn_ref,
                y_ref, state_ref, tail_ref, ext_ref):
    @pl.when(pl.program_id(0) == 0)
    def _():
        state_ref[...] = jnp.zeros_like(state_ref)
        tail_ref[...] = jnp.zeros_like(tail_ref)

    def conv_silu(cur_ref, col0, width):
        cur = cur_ref[...]
        ext_ref[0:SUBLANES, 0:width] = tail_ref[:, col0:col0 + width]
        ext_ref[SUBLANES:SUBLANES + CHUNK, 0:width] = cur
        acc = cb_ref[:, col0:col0 + width] + cw_ref[SSD_CONV - 1:SSD_CONV, col0:col0 + width] * cur
        for j in range(SSD_CONV - 1):
            off = SUBLANES - (SSD_CONV - 1) + j
            acc = acc + cw_ref[j:j + 1, col0:col0 + width] * ext_ref[off:off + CHUNK, 0:width]
        tail_ref[:, col0:col0 + width] = cur[CHUNK - SUBLANES:CHUNK, :]
        return _silu(acc)

    xs = conv_silu(x_ref, 0, SSD_DI)
    bm = conv_silu(b_ref, SSD_DI, SSD_GROUPS * SSD_STATE)
    cm = conv_silu(c_ref, SSD_DI + SSD_GROUPS * SSD_STATE, SSD_GROUPS * SSD_STATE)

    raw = dt_ref[...] + dtb_ref[...]
    dt = jnp.maximum(raw, 0.0) + jnp.log1p(jnp.exp(-jnp.abs(raw)))
    a = dt * aneg_ref[...]
    row = lax.broadcasted_iota(I32, (CHUNK, CHUNK), 0)
    col = lax.broadcasted_iota(I32, (CHUNK, CHUNK), 1)
    causal = row >= col
    a_cum = jnp.dot(causal.astype(F32), a, precision=lax.Precision.HIGHEST, preferred_element_type=F32)
    a_cum_t = a_cum.T
    a_last = a_cum[CHUNK - 1:CHUNK, :]
    dt_to_end = dt * jnp.exp(a_last - a_cum)
    e_acum = jnp.exp(a_cum)
    chunk_decay = jnp.exp(a_last)

    lane = lax.broadcasted_iota(I32, (CHUNK, LANES), 1)
    lo_half = lane < SSD_HEADDIM
    hpg = SSD_HEADS // SSD_GROUPS
    gw = hpg * SSD_HEADDIM
    for g in range(SSD_GROUPS):
        bg = bm[:, g * SSD_STATE:(g + 1) * SSD_STATE].astype(BF16)
        cg = cm[:, g * SSD_STATE:(g + 1) * SSD_STATE].astype(BF16)
        cb = _dot_nt(cg, bg)
        prev = state_ref[:, g * gw:(g + 1) * gw]
        y_off = _dot(cg, prev.astype(BF16))
        y_parts = []
        xdec_parts = []
        dec_parts = []
        for p in range(2):
            j = 2 * g + p
            xj = xs[:, j * LANES:(j + 1) * LANES]
            x_dt = (xj * _pair_bcast(dt, j)).astype(BF16)
            xdec_parts.append((xj * _pair_bcast(dt_to_end, j)).astype(BF16))
            yd = []
            for e in range(2):
                h = 2 * j + e
                diff = jnp.broadcast_to(a_cum[:, h:h + 1], (CHUNK, CHUNK)) - a_cum_t[h:h + 1, :]
                lmat = (cb * jnp.exp(jnp.where(causal, diff, -jnp.inf))).astype(BF16)
                yd.append(_dot(lmat, x_dt))
            y_diag = jnp.where(lo_half, yd[0], yd[1])
            y_parts.append(y_diag + y_off[:, p * LANES:(p + 1) * LANES] * _pair_bcast(e_acum, j)
                           + dskip_ref[:, j * LANES:(j + 1) * LANES] * xj)
            dec_parts.append(_pair_bcast(chunk_decay, j))
        xdec = jnp.concatenate(xdec_parts, axis=1)
        new_state = _dot_tn(bg, xdec)
        state_ref[:, g * gw:(g + 1) * gw] = jnp.concatenate(dec_parts, axis=1) * prev + new_state
        yg = jnp.concatenate(y_parts, axis=1) * _silu(z_ref[:, g * gw:(g + 1) * gw])
        ms = jnp.mean(yg * yg, axis=-1, keepdims=True)
        y_ref[:, g * gw:(g + 1) * gw] = (yg * lax.rsqrt(ms + NORM_EPS) * gn_ref[:, g * gw:(g + 1) * gw]).astype(BF16)


def _ssd_core(proj, conv_w, conv_b, dt_bias, a_log, d_skip, gnorm):
    L = proj.shape[0]
    nb = SSD_GROUPS * SSD_STATE
    pad = LANES - SSD_HEADS
    dtb = jnp.pad(dt_bias, (0, pad)).reshape(1, LANES)
    aneg = jnp.pad(-jnp.exp(a_log.astype(F32)), (0, pad)).reshape(1, LANES)
    dskip = jnp.repeat(d_skip.astype(F32), SSD_HEADDIM).reshape(1, SSD_DI)
    full = lambda shape: pl.BlockSpec(shape, lambda c: (0, 0))
    return pl.pallas_call(
        _ssd_kernel,
        grid=(L // CHUNK,),
        in_specs=[
            pl.BlockSpec((CHUNK, SSD_DI), lambda c: (c, 0)),
            pl.BlockSpec((CHUNK, SSD_DI), lambda c: (c, 1)),
            pl.BlockSpec((CHUNK, nb), lambda c: (c, 2 * SSD_DI // nb)),
            pl.BlockSpec((CHUNK, nb), lambda c: (c, 2 * SSD_DI // nb + 1)),
            pl.BlockSpec((CHUNK, LANES), lambda c: (c, (SSD_DI + SSD_CONV_DIM) // LANES)),
            full((SSD_CONV, SSD_CONV_DIM)), full((1, SSD_CONV_DIM)), full((1, LANES)), full((1, LANES)),
            full((1, SSD_DI)), full((1, SSD_DI)),
        ],
        out_specs=pl.BlockSpec((CHUNK, SSD_DI), lambda c: (c, 0)),
        out_shape=jax.ShapeDtypeStruct((L, SSD_DI), BF16),
        scratch_shapes=[
            pltpu.VMEM((SSD_STATE, SSD_DI), F32),
            pltpu.VMEM((SUBLANES, SSD_CONV_DIM), F32),
            pltpu.VMEM((SUBLANES + CHUNK, SSD_DI), F32),
        ],
        compiler_params=_cparams(("arbitrary",), VMEM_LIMIT_BYTES),
        name="ssd_core",
    )(proj, proj, proj, proj, proj, conv_w, conv_b.reshape(1, SSD_CONV_DIM), dtb, aneg, dskip,
      gnorm.reshape(1, SSD_DI))


def _ssd_layer(h, norm_g, w_in, conv_w, conv_b, dt_bias, a_log, d_skip, gnorm, w_out, final_g=None):
    w = jnp.pad(w_in, ((0, 0), (0, SSD_IN_PAD - SSD_IN))).astype(BF16)
    proj = _norm_matmul(h, norm_g, w, tm=512, tn=SSD_IN_PAD // 7)
    y = _ssd_core(proj, conv_w, conv_b, dt_bias, a_log, d_skip, gnorm)
    return _out_proj(y, w_out.astype(BF16), h, final_g)


def _ret_kernel(q_ref, k_ref, v_ref, g_ref, cos_ref, sin_ref, gn_ref, y_ref, state_ref):
    @pl.when(pl.program_id(0) == 0)
    def _():
        state_ref[...] = jnp.zeros_like(state_ref)

    cos = cos_ref[...]
    sin = sin_ref[...]
    half = RET_DK // 2
    rowf = lax.broadcasted_iota(I32, (CHUNK, LANES), 0).astype(F32)
    row = lax.broadcasted_iota(I32, (CHUNK, CHUNK), 0)
    col = lax.broadcasted_iota(I32, (CHUNK, CHUNK), 1)
    diff = (row - col).astype(F32)

    def rope(ref, h, scale):
        x1 = ref[:, h * RET_DK:h * RET_DK + half]
        x2 = ref[:, h * RET_DK + half:(h + 1) * RET_DK]
        return (x1 * cos - x2 * sin) * scale, (x2 * cos + x1 * sin) * scale

    for h in range(RET_HEADS):
        log_g = float(np.log1p(-np.exp2(np.float32(-5.0 - h))))
        q1, q2 = rope(q_ref, h, 1.0)
        k1, k2 = rope(k_ref, h, RET_DK ** -0.5)
        qr = jnp.concatenate([q1, q2], axis=1).astype(BF16)
        kr = jnp.concatenate([k1, k2], axis=1).astype(BF16)
        v = v_ref[:, h * RET_DV:(h + 1) * RET_DV].astype(BF16)
        intra = jnp.where(diff >= 0, jnp.exp(jnp.maximum(diff, 0.0) * log_g), 0.0)
        scores = (_dot_nt(qr, kr) * intra).astype(BF16)
        o = _dot(scores, v)
        q_decay = jnp.exp((rowf + 1.0) * log_g)
        q_dec = jnp.concatenate([q1 * q_decay, q2 * q_decay], axis=1).astype(BF16)
        prev = state_ref[h]
        o = o + _dot(q_dec, prev.astype(BF16))
        k_decay = jnp.exp((CHUNK - 1.0 - rowf) * log_g)
        k_dec = jnp.concatenate([k1 * k_decay, k2 * k_decay], axis=1).astype(BF16)
        state_ref[h] = math.exp(CHUNK * log_g) * prev + _dot_tn(k_dec, v)
        ms = jnp.mean(o * o, axis=-1, keepdims=True)
        on = o * lax.rsqrt(ms + NORM_EPS) * gn_ref[:, h * RET_DV:(h + 1) * RET_DV]
        y_ref[:, h * RET_DV:(h + 1) * RET_DV] = (on * _silu(g_ref[:, h * RET_DV:(h + 1) * RET_DV])).astype(BF16)


def _ret_core(proj, cos, sin, gnorm):
    L = proj.shape[0]
    qk = RET_HEADS * RET_DK
    return pl.pallas_call(
        _ret_kernel,
        grid=(L // CHUNK,),
        in_specs=[
            pl.BlockSpec((CHUNK, qk), lambda c: (c, 0)),
            pl.BlockSpec((CHUNK, qk), lambda c: (c, 1)),
            pl.BlockSpec((CHUNK, RET_DVTOT), lambda c: (c, 1)),
            pl.BlockSpec((CHUNK, RET_DVTOT), lambda c: (c, 2)),
            pl.BlockSpec((CHUNK, LANES), lambda c: (c, 0)),
            pl.BlockSpec((CHUNK, LANES), lambda c: (c, 0)),
            pl.BlockSpec((1, RET_DVTOT), lambda c: (0, 0)),
        ],
        out_specs=pl.BlockSpec((CHUNK, RET_DVTOT), lambda c: (c, 0)),
        out_shape=jax.ShapeDtypeStruct((L, RET_DVTOT), BF16),
        scratch_shapes=[pltpu.VMEM((RET_HEADS, RET_DK, RET_DV), F32)],
        compiler_params=_cparams(("arbitrary",), VMEM_LIMIT_BYTES),
        name="retention_core",
    )(proj, proj, proj, proj, cos, sin, gnorm.reshape(1, RET_DVTOT))


def _ret_layer(h, norm_g, w_in, gnorm, w_out, cos, sin):
    proj = _norm_matmul(h, norm_g, w_in.astype(BF16), tm=512, tn=1024)
    y = _ret_core(proj, cos, sin, gnorm)
    return _out_proj(y, w_out.astype(BF16), h)


def _dsa_rope(x, c, s1, s2):
    half = DSA_HEADDIM // ROPE_FRACTION // 2
    parts = []
    for b in range(x.shape[1] // LANES):
        xb = x[:, b * LANES:(b + 1) * LANES]
        parts.append(xb * c + pltpu.roll(xb, LANES - half, 1) * s1 + pltpu.roll(xb, half, 1) * s2)
    return parts[0] if len(parts) == 1 else jnp.concatenate(parts, axis=1)


def _dsa_prep_kernel(q_ref, qi_ref, kv_ref, kw_ref, c_ref, s1_ref, s2_ref, kn_ref,
                     qo_ref, qio_ref, kd_ref, vo_ref, kid_ref, wo_ref):
    c = c_ref[...]
    s1 = s1_ref[...]
    s2 = s2_ref[...]
    rows = c.shape[0]
    lane = lax.broadcasted_iota(I32, (rows, LANES), 1)
    lo_half = lane < DSA_HEADDIM
    qo_ref[...] = (_dsa_rope(q_ref[...], c, s1, s2) * DSA_HEADDIM ** -0.5).astype(BF16)
    qio_ref[...] = (_dsa_rope(qi_ref[...], c, s1, s2) * IDX_DIM ** -0.5).astype(BF16)
    kwidth = DSA_KV_HEADS * DSA_HEADDIM
    kr = _dsa_rope(kv_ref[:, 0:kwidth], c, s1, s2)
    for n in range(DSA_KV_HEADS):
        xb = kr[:, (n // 2) * LANES:(n // 2 + 1) * LANES]
        sw = pltpu.roll(xb, DSA_HEADDIM, 1)
        dup = jnp.where(lo_half, xb, sw) if n % 2 == 0 else jnp.where(lo_half, sw, xb)
        kd_ref[n] = dup.astype(BF16)
    vo_ref[...] = kv_ref[:, kwidth:2 * kwidth].astype(BF16)
    kw = kw_ref[...]
    ki = jnp.where(lo_half, kw, 0.0)
    ms = jnp.sum(ki * ki, axis=-1, keepdims=True) * (1.0 / IDX_DIM)
    kin = ki * lax.rsqrt(ms + NORM_EPS) * kn_ref[...]
    kir = jnp.where(lo_half, _dsa_rope(kin, c, s1, s2), 0.0)
    kid_ref[...] = (kir + pltpu.roll(kir, DSA_HEADDIM, 1)).astype(BF16)
    wo_ref[...] = jnp.where(lane < IDX_HEADS, pltpu.roll(kw, DSA_HEADDIM, 1) * IDX_HEADS ** -0.5, 0.0)


def _dsa_prep(proj, c, s1, s2, idx_knorm, tl=512):
    L = proj.shape[0]
    kn = jnp.pad(idx_knorm.astype(F32), (0, LANES - IDX_DIM)).reshape(1, LANES)
    tok = lambda w, j: pl.BlockSpec((tl, w), lambda i: (i, j))
    return pl.pallas_call(
        _dsa_prep_kernel,
        grid=(L // tl,),
        in_specs=[
            tok(DSA_WIDTH, 0), tok(DSA_WIDTH, 1), tok(512, 6), tok(LANES, 28),
            tok(LANES, 0), tok(LANES, 0), tok(LANES, 0),
            pl.BlockSpec((1, LANES), lambda i: (0, 0)),
        ],
        out_specs=[
            tok(DSA_WIDTH, 0), tok(DSA_WIDTH, 0),
            pl.BlockSpec((DSA_KV_HEADS, tl, LANES), lambda i: (0, i, 0)),
            tok(256, 0), tok(LANES, 0), tok(LANES, 0),
        ],
        out_shape=[
            jax.ShapeDtypeStruct((L, DSA_WIDTH), BF16),
            jax.ShapeDtypeStruct((L, DSA_WIDTH), BF16),
            jax.ShapeDtypeStruct((DSA_KV_HEADS, L, LANES), BF16),
            jax.ShapeDtypeStruct((L, 256), BF16),
            jax.ShapeDtypeStruct((L, LANES), BF16),
            jax.ShapeDtypeStruct((L, LANES), F32),
        ],
        compiler_params=_cparams(("parallel",), VMEM_LIMIT_BYTES),
        name="dsa_prep",
    )(proj, proj, proj, proj, c, s1, s2, kn)


def _float_key(score):
    bits = lax.bitcast_convert_type(score, I32)
    key = jnp.where(bits < 0, bits ^ jnp.int32(0x7FFFFFFF), bits)
    return jnp.where(score == 0.0, 0, key)


def _dsa_main_kernel(q_ref, qi_ref, w_ref, g_ref, kid_ref, kd_ref, v_ref, y_ref,
                     key_ref, qis_ref, wb_ref, qs_ref, m_ref, l_ref, acc_ref, *, topk):
    i = pl.program_id(0)
    tq = DSA_QBLOCK
    tk = DSA_KTILE
    n_tiles = (i * tq + tq + tk - 1) // tk
    lane = lax.broadcasted_iota(I32, (tq, LANES), 1)
    lo_half = lane < DSA_HEADDIM

    for h in range(IDX_HEADS):
        blk = qi_ref[:, (h // 2) * LANES:(h // 2 + 1) * LANES]
        keep = lo_half if h % 2 == 0 else jnp.logical_not(lo_half)
        qis_ref[h * tq:(h + 1) * tq, :] = jnp.where(keep, blk, jnp.zeros_like(blk))
        wb_ref[h] = jnp.broadcast_to(w_ref[:, h:h + 1], (tq, LANES))
    for h in range(DSA_HEADS):
        blk = q_ref[:, (h // 2) * LANES:(h // 2 + 1) * LANES]
        keep = lo_half if h % 2 == 0 else jnp.logical_not(lo_half)
        qs_ref[h * tq:(h + 1) * tq, :] = jnp.where(keep, blk, jnp.zeros_like(blk))

    q_pos = i * tq + lax.broadcasted_iota(I32, (tq, tk), 0)
    col_in_tile = lax.broadcasted_iota(I32, (tq, tk), 1)

    def score_tile(t, carry):
        k0 = pl.multiple_of(t * tk, tk)
        s_all = _dot_nt(qis_ref[...], kid_ref[pl.ds(k0, tk), :])
        acc = jnp.zeros((tq, tk), F32)
        for h in range(IDX_HEADS):
            wb = wb_ref[h]
            wt = jnp.concatenate([wb] * (tk // LANES), axis=1)
            acc = acc + wt * jnp.maximum(s_all[h * tq:(h + 1) * tq, :], 0.0)
        key = _float_key(acc)
        key = jnp.where(k0 + col_in_tile <= q_pos, key, INT_MIN)
        key_ref[:, pl.ds(k0, tk)] = key
        return carry

    lax.fori_loop(0, n_tiles, score_tile, 0)

    def bisect(b, thr):
        cand = thr + lax.shift_left(jnp.int32(1), jnp.int32(31) - b)
        cand_b = jnp.broadcast_to(cand, (tq, LANES))

        def count_tile(t, cnt):
            k0 = pl.multiple_of(t * tk, tk)
            for s in range(tk // LANES):
                kk = key_ref[:, pl.ds(k0 + s * LANES, LANES)]
                cnt = cnt + jnp.where(kk >= cand_b, 1.0, 0.0)
            return cnt

        cnt = lax.fori_loop(0, n_tiles, count_tile, jnp.zeros((tq, LANES), F32))
        total = jnp.sum(cnt, axis=-1, keepdims=True)
        return jnp.where(total >= float(topk), cand, thr)

    thr = lax.fori_loop(0, 32, bisect, jnp.full((tq, 1), INT_MIN, I32))
    thr = jnp.maximum(thr, INT_MIN + 1)
    thr_b = jnp.broadcast_to(thr, (tq, tk))

    rows = DSA_GQA * tq
    for n in range(DSA_KV_HEADS):
        m_ref[...] = jnp.full(m_ref.shape, NEG_BIG, F32)
        l_ref[...] = jnp.zeros(l_ref.shape, F32)
        acc_ref[...] = jnp.zeros(acc_ref.shape, F32)

        def attend(t, carry):
            k0 = pl.multiple_of(t * tk, tk)
            sel = key_ref[:, pl.ds(k0, tk)] >= thr_b
            sel4 = jnp.concatenate([sel] * DSA_GQA, axis=0)
            logits = _dot_nt(qs_ref[n * rows:(n + 1) * rows, :], kd_ref[n, pl.ds(k0, tk), :])
            s = jnp.where(sel4, logits, NEG_BIG)
            m_old = m_ref[...]
            m_new = jnp.maximum(m_old, jnp.max(s, axis=-1, keepdims=True))
            p = jnp.where(sel4, jnp.exp(s - m_new[:, 0:1]), 0.0)
            alpha = jnp.exp(m_old - m_new)
            l_ref[...] = alpha * l_ref[...] + jnp.sum(p, axis=-1, keepdims=True)
            vt = v_ref[pl.ds(k0, tk), (n // 2) * LANES:(n // 2 + 1) * LANES]
            acc_ref[...] = alpha * acc_ref[...] + _dot(p.astype(BF16), vt)
            m_ref[...] = m_new
            return carry

        lax.fori_loop(0, n_tiles, attend, 0)
        o = acc_ref[...] / l_ref[...]
        for pr in range(DSA_GQA // 2):
            h0 = n * DSA_GQA + 2 * pr
            a = o[(2 * pr) * tq:(2 * pr + 1) * tq, :]
            b = o[(2 * pr + 1) * tq:(2 * pr + 2) * tq, :]
            if n % 2 == 0:
                blk = jnp.where(lo_half, a, pltpu.roll(b, DSA_HEADDIM, 1))
            else:
                blk = jnp.where(lo_half, pltpu.roll(a, DSA_HEADDIM, 1), b)
            cols = slice((h0 // 2) * LANES, (h0 // 2 + 1) * LANES)
            y_ref[:, cols] = (blk * _silu(g_ref[:, cols])).astype(BF16)


def _dsa_main(proj, qr, qir, kdup, vb, kidup, wi, topk):
    L = proj.shape[0]
    tq = DSA_QBLOCK
    resident = lambda shape: pl.BlockSpec(shape, lambda i: (0,) * len(shape), pipeline_mode=pl.Buffered(1))
    return pl.pallas_call(
        functools.partial(_dsa_main_kernel, topk=topk),
        grid=(L // tq,),
        in_specs=[
            pl.BlockSpec((tq, DSA_WIDTH), lambda i: (i, 0)),
            pl.BlockSpec((tq, DSA_WIDTH), lambda i: (i, 0)),
            pl.BlockSpec((tq, LANES), lambda i: (i, 0)),
            pl.BlockSpec((tq, DSA_WIDTH), lambda i: (i, 2)),
            resident((L, LANES)),
            resident((DSA_KV_HEADS, L, LANES)),
            resident((L, 256)),
        ],
        out_specs=pl.BlockSpec((tq, DSA_WIDTH), lambda i: (i, 0)),
        out_shape=jax.ShapeDtypeStruct((L, DSA_WIDTH), BF16),
        scratch_shapes=[
            pltpu.VMEM((tq, L), I32),
            pltpu.VMEM((IDX_HEADS * tq, LANES), BF16),
            pltpu.VMEM((IDX_HEADS, tq, LANES), F32),
            pltpu.VMEM((DSA_HEADS * tq, LANES), BF16),
            pltpu.VMEM((DSA_GQA * tq, LANES), F32),
            pltpu.VMEM((DSA_GQA * tq, LANES), F32),
            pltpu.VMEM((DSA_GQA * tq, LANES), F32),
        ],
        compiler_params=_cparams(("arbitrary",), VMEM_LIMIT_BYTES),
        name="dsa_index_attend",
    )(qr, qir, wi, proj, kidup, kdup, vb)


def _dsa_layer(h, norm_g, w_in, idx_knorm, w_out, c, s1, s2):
    L = h.shape[0]
    q, k, v, g, qi, ki, wi = jnp.split(w_in, np.cumsum([1024, 256, 256, 1024, 1024, 64, 16])[:-1].tolist(), axis=1)
    w = jnp.concatenate([q, qi, g, k, v, ki, wi], axis=1)
    w = jnp.pad(w, ((0, 0), (0, DSA_IN_PAD - w.shape[1]))).astype(BF16)
    proj = _norm_matmul(h, norm_g, w, tm=512, tn=768)
    qr, qir, kdup, vb, kidup, wsc = _dsa_prep(proj, c, s1, s2, idx_knorm)
    y = _dsa_main(proj, qr, qir, kdup, vb, kidup, wsc, topk=min(TOPK_MAX, L // 4))
    return _out_proj(y, w_out.astype(BF16), h)


def kernel(x, positions, l0_norm, l0_w_in, l0_conv_w, l0_conv_b, l0_dt_bias, l0_a_log, l0_d_skip, l0_gnorm, l0_w_out, l1_norm, l1_w_in, l1_gnorm, l1_w_out, l2_norm, l2_w_in, l2_idx_knorm, l2_w_out, l3_norm, l3_w_in, l3_conv_w, l3_conv_b, l3_dt_bias, l3_a_log, l3_d_skip, l3_gnorm, l3_w_out, final_norm):
    b, L, d = x.shape
    outs = []
    for bi in range(b):
        h = x[bi]
        rc, rs, dc, ds1, ds2 = _rope_tables(positions[bi].astype(F32))
        h = _ssd_layer(h, l0_norm, l0_w_in, l0_conv_w, l0_conv_b, l0_dt_bias, l0_a_log, l0_d_skip, l0_gnorm, l0_w_out)
        h = _ret_layer(h, l1_norm, l1_w_in, l1_gnorm, l1_w_out, rc, rs)
        h = _dsa_layer(h, l2_norm, l2_w_in, l2_idx_knorm, l2_w_out, dc, ds1, ds2)
        h = _ssd_layer(h, l3_norm, l3_w_in, l3_conv_w, l3_conv_b, l3_dt_bias, l3_a_log, l3_d_skip, l3_gnorm, l3_w_out,
                       final_g=final_norm)
        outs.append(h)
    return jnp.stack(outs, axis=0)
```

```python
import functools
import math

import numpy as np
import jax
import jax.numpy as jnp
from jax import lax
from jax.experimental import pallas as pl
from jax.experimental.pallas import tpu as pltpu

F32 = jnp.float32
BF16 = jnp.bfloat16
I32 = jnp.int32

LANES = 128
SUBLANES = 8
VMEM_LIMIT_BYTES = 56 * 1024 * 1024

D_MODEL = 1024
NORM_EPS = 1e-6
CHUNK = 128

SSD_DI = 2048
SSD_HEADDIM = 64
SSD_HEADS = 32
SSD_GROUPS = 8
SSD_STATE = 128
SSD_CONV = 4
SSD_CONV_DIM = 4096
SSD_IN = SSD_DI + SSD_CONV_DIM + SSD_HEADS
SSD_IN_PAD = SSD_DI + SSD_CONV_DIM + LANES

RET_HEADS = 4
RET_DK = 256
RET_DV = 512
RET_DVTOT = 2048
RET_THETA = 10000.0
RET_IN = 2 * RET_HEADS * RET_DK + 2 * RET_DVTOT

DSA_HEADS = 16
DSA_HEADDIM = 64
DSA_KV_HEADS = 4
DSA_GQA = 4
DSA_WIDTH = 1024
IDX_HEADS = 16
IDX_DIM = 64
TOPK_MAX = 256
ROPE_THETA = 500000.0
ROPE_FRACTION = 4
DSA_IN_PAD = 3840
DSA_QBLOCK = 128
DSA_KTILE = 256

INT_MIN = -(2 ** 31)
NEG_BIG = -1e30
M_INIT = -1e29


def _cparams(sem, vmem=None):
    return pltpu.CompilerParams(dimension_semantics=sem, vmem_limit_bytes=vmem)


def _sigmoid(x):
    return 1.0 / (1.0 + jnp.exp(-x))


def _silu(x):
    return x * _sigmoid(x)


def _dot(a, b):
    return jnp.dot(a, b, preferred_element_type=F32)


def _dot_nt(a, b):
    return lax.dot_general(a, b, (((1,), (1,)), ((), ())), preferred_element_type=F32)


def _dot_tn(a, b):
    return lax.dot_general(a, b, (((0,), (0,)), ((), ())), preferred_element_type=F32)


def _norm_matmul_kernel(h_ref, g_ref, w_ref, o_ref, hn_ref):
    @pl.when(pl.program_id(1) == 0)
    def _():
        x = h_ref[...]
        ms = jnp.mean(x * x, axis=-1, keepdims=True)
        hn_ref[...] = (x * lax.rsqrt(ms + NORM_EPS) * g_ref[...]).astype(BF16)

    o_ref[...] = _dot(hn_ref[...], w_ref[...])


def _norm_matmul(h, g, w_bf16, tm, tn):
    L, d = h.shape
    n = w_bf16.shape[1]
    return pl.pallas_call(
        _norm_matmul_kernel,
        grid=(L // tm, n // tn),
        in_specs=[
            pl.BlockSpec((tm, d), lambda i, j: (i, 0)),
            pl.BlockSpec((1, d), lambda i, j: (0, 0)),
            pl.BlockSpec((d, tn), lambda i, j: (0, j)),
        ],
        out_specs=pl.BlockSpec((tm, tn), lambda i, j: (i, j)),
        out_shape=jax.ShapeDtypeStruct((L, n), F32),
        scratch_shapes=[pltpu.VMEM((tm, d), BF16)],
        compiler_params=_cparams(("parallel", "arbitrary"), VMEM_LIMIT_BYTES),
        name="norm_in_proj",
    )(h, g.reshape(1, d), w_bf16)


def _out_proj_kernel(y_ref, w_ref, h_ref, o_ref):
    o_ref[...] = h_ref[...] + _dot(y_ref[...], w_ref[...])


def _out_proj_final_kernel(y_ref, w_ref, h_ref, g_ref, o_ref):
    x = h_ref[...] + _dot(y_ref[...], w_ref[...])
    ms = jnp.mean(x * x, axis=-1, keepdims=True)
    o_ref[...] = x * lax.rsqrt(ms + NORM_EPS) * g_ref[...]


def _out_proj(y_bf16, w_bf16, h, final_g=None, tm=512):
    L, k = y_bf16.shape
    d = w_bf16.shape[1]
    in_specs = [
        pl.BlockSpec((tm, k), lambda i: (i, 0)),
        pl.BlockSpec((k, d), lambda i: (0, 0)),
        pl.BlockSpec((tm, d), lambda i: (i, 0)),
    ]
    args = [y_bf16, w_bf16, h]
    body = _out_proj_kernel
    if final_g is not None:
        in_specs.append(pl.BlockSpec((1, d), lambda i: (0, 0)))
        args.append(final_g.reshape(1, d))
        body = _out_proj_final_kernel
    return pl.pallas_call(
        body,
        grid=(L // tm,),
        in_specs=in_specs,
        out_specs=pl.BlockSpec((tm, d), lambda i: (i, 0)),
        out_shape=jax.ShapeDtypeStruct((L, d), F32),
        compiler_params=_cparams(("parallel",), VMEM_LIMIT_BYTES),
        name="out_proj_residual",
    )(*args)


def _rope_table_kernel(pos_ref, inv_ret_ref, inv_dsa_ref, rc_ref, rs_ref, dc_ref, ds1_ref, ds2_ref):
    pos = pos_ref[...]
    ang = pos * inv_ret_ref[...]
    rc_ref[...] = jnp.cos(ang)
    rs_ref[...] = jnp.sin(ang)
    ang_d = pos * inv_dsa_ref[...]
    lane = lax.broadcasted_iota(I32, ang_d.shape, 1) % DSA_HEADDIM
    half = DSA_HEADDIM // ROPE_FRACTION // 2
    c = jnp.cos(ang_d)
    s = jnp.sin(ang_d)
    dc_ref[...] = jnp.where(lane < 2 * half, c, 1.0)
    ds1_ref[...] = jnp.where(lane < half, -s, 0.0)
    ds2_ref[...] = jnp.where((lane >= half) & (lane < 2 * half), s, 0.0)


def _rope_tables(positions_f32, tl=1024):
    L = positions_f32.shape[0]
    inv_ret = 1.0 / (jnp.float32(RET_THETA) ** jnp.linspace(0.0, 1.0, RET_DK // 2, dtype=F32))
    r = DSA_HEADDIM // ROPE_FRACTION
    inv_dsa = jnp.float32(ROPE_THETA) ** (-jnp.arange(0, r, 2, dtype=F32) / r)
    inv_dsa_lanes = jnp.tile(inv_dsa, LANES // inv_dsa.shape[0])
    tab = jax.ShapeDtypeStruct((L, LANES), F32)
    spec = pl.BlockSpec((tl, LANES), lambda i: (i, 0))
    cst = pl.BlockSpec((1, LANES), lambda i: (0, 0))
    return pl.pallas_call(
        _rope_table_kernel,
        grid=(L // tl,),
        in_specs=[pl.BlockSpec((tl, 1), lambda i: (i, 0)), cst, cst],
        out_specs=[spec] * 5,
        out_shape=[tab] * 5,
        compiler_params=_cparams(("parallel",)),
        name="rope_tables",
    )(positions_f32.reshape(L, 1), inv_ret.reshape(1, LANES), inv_dsa_lanes.reshape(1, LANES))


def _pair_bcast(m, j):
    rows = m.shape[0]
    lane = lax.broadcasted_iota(I32, (rows, LANES), 1)
    lo = jnp.broadcast_to(m[:, 2 * j:2 * j + 1], (rows, LANES))
    hi = jnp.broadcast_to(m[:, 2 * j + 1:2 * j + 2], (rows, LANES))
    return jnp.where(lane < SSD_HEADDIM, lo, hi)


def _ssd_kernel(z_ref, x_ref, b_ref, c_ref, dt_ref, cw_ref, cb_ref, dtb_ref, aneg_ref, dskip_ref, gn_ref,
                y_ref, state_ref, tail_ref, ext_ref):
    @pl.when(pl.program_id(0) == 0)
    def _():
        state_ref[...] = jnp.zeros_like(state_ref)
        tail_ref[...] = jnp.zeros_like(tail_ref)

    def conv_silu(cur_ref, col0, width):
        cur = cur_ref[...]
        ext_ref[0:SUBLANES, 0:width] = tail_ref[:, col0:col0 + width]
        ext_ref[SUBLANES:SUBLANES + CHUNK, 0:width] = cur
        acc = cb_ref[:, col0:col0 + width] + cw_ref[SSD_CONV - 1:SSD_CONV, col0:col0 + width] * cur
        for j in range(SSD_CONV - 1):
            off = SUBLANES - (SSD_CONV - 1) + j
            acc = acc + cw_ref[j:j + 1, col0:col0 + width] * ext_ref[off:off + CHUNK, 0:width]
        tail_ref[:, col0:col0 + width] = cur[CHUNK - SUBLANES:CHUNK, :]
        return _silu(acc)

    xs = conv_silu(x_ref, 0, SSD_DI)
    bm = conv_silu(b_ref, SSD_DI, SSD_GROUPS * SSD_STATE)
    cm = conv_silu(c_ref, SSD_DI + SSD_GROUPS * SSD_STATE, SSD_GROUPS * SSD_STATE)

    raw = dt_ref[...] + dtb_ref[...]
    dt = jnp.maximum(raw, 0.0) + jnp.log1p(jnp.exp(-jnp.abs(raw)))
    a = dt * aneg_ref[...]
    row = lax.broadcasted_iota(I32, (CHUNK, CHUNK), 0)
    col = lax.broadcasted_iota(I32, (CHUNK, CHUNK), 1)
    causal = row >= col
    a_cum = jnp.dot(causal.astype(F32), a, precision=lax.Precision.HIGHEST, preferred_element_type=F32)
    a_cum_t = a_cum.T
    a_last = a_cum[CHUNK - 1:CHUNK, :]
    dt_to_end = dt * jnp.exp(a_last - a_cum)
    e_acum = jnp.exp(a_cum)
    chunk_decay = jnp.exp(a_last)

    lane = lax.broadcasted_iota(I32, (CHUNK, LANES), 1)
    lo_half = lane < SSD_HEADDIM
    hpg = SSD_HEADS // SSD_GROUPS
    gw = hpg * SSD_HEADDIM
    for g in range(SSD_GROUPS):
        bg = bm[:, g * SSD_STATE:(g + 1) * SSD_STATE].astype(BF16)
        cg = cm[:, g * SSD_STATE:(g + 1) * SSD_STATE].astype(BF16)
        cb = _dot_nt(cg, bg)
        prev = state_ref[:, g * gw:(g + 1) * gw]
        y_off = _dot(cg, prev.astype(BF16))
        y_parts = []
        xdec_parts = []
        dec_parts = []
        for p in range(2):
            j = 2 * g + p
            xj = xs[:, j * LANES:(j + 1) * LANES]
            x_dt = (xj * _pair_bcast(dt, j)).astype(BF16)
            xdec_parts.append((xj * _pair_bcast(dt_to_end, j)).astype(BF16))
            yd = []
            for e in range(2):
                h = 2 * j + e
                diff = jnp.broadcast_to(a_cum[:, h:h + 1], (CHUNK, CHUNK)) - a_cum_t[h:h + 1, :]
                lmat = (cb * jnp.exp(jnp.where(causal, diff, -jnp.inf))).astype(BF16)
                yd.append(_dot(lmat, x_dt))
            y_diag = jnp.where(lo_half, yd[0], yd[1])
            y_parts.append(y_diag + y_off[:, p * LANES:(p + 1) * LANES] * _pair_bcast(e_acum, j)
                           + dskip_ref[:, j * LANES:(j + 1) * LANES] * xj)
            dec_parts.append(_pair_bcast(chunk_decay, j))
        xdec = jnp.concatenate(xdec_parts, axis=1)
        new_state = _dot_tn(bg, xdec)
        state_ref[:, g * gw:(g + 1) * gw] = jnp.concatenate(dec_parts, axis=1) * prev + new_state
        yg = jnp.concatenate(y_parts, axis=1) * _silu(z_ref[:, g * gw:(g + 1) * gw])
        ms = jnp.mean(yg * yg, axis=-1, keepdims=True)
        y_ref[:, g * gw:(g + 1) * gw] = (yg * lax.rsqrt(ms + NORM_EPS) * gn_ref[:, g * gw:(g + 1) * gw]).astype(BF16)


def _ssd_core(proj, conv_w, conv_b, dt_bias, a_log, d_skip, gnorm):
    L = proj.shape[0]
    nb = SSD_GROUPS * SSD_STATE
    pad = LANES - SSD_HEADS
    dtb = jnp.pad(dt_bias, (0, pad)).reshape(1, LANES)
    aneg = jnp.pad(-jnp.exp(a_log.astype(F32)), (0, pad)).reshape(1, LANES)
    dskip = jnp.repeat(d_skip.astype(F32), SSD_HEADDIM).reshape(1, SSD_DI)
    full = lambda shape: pl.BlockSpec(shape, lambda c: (0, 0))
    return pl.pallas_call(
        _ssd_kernel,
        grid=(L // CHUNK,),
        in_specs=[
            pl.BlockSpec((CHUNK, SSD_DI), lambda c: (c, 0)),
            pl.BlockSpec((CHUNK, SSD_DI), lambda c: (c, 1)),
            pl.BlockSpec((CHUNK, nb), lambda c: (c, 2 * SSD_DI // nb)),
            pl.BlockSpec((CHUNK, nb), lambda c: (c, 2 * SSD_DI // nb + 1)),
            pl.BlockSpec((CHUNK, LANES), lambda c: (c, (SSD_DI + SSD_CONV_DIM) // LANES)),
            full((SSD_CONV, SSD_CONV_DIM)), full((1, SSD_CONV_DIM)), full((1, LANES)), full((1, LANES)),
            full((1, SSD_DI)), full((1, SSD_DI)),
        ],
        out_specs=pl.BlockSpec((CHUNK, SSD_DI), lambda c: (c, 0)),
        out_shape=jax.ShapeDtypeStruct((L, SSD_DI), BF16),
        scratch_shapes=[
            pltpu.VMEM((SSD_STATE, SSD_DI), F32),
            pltpu.VMEM((SUBLANES, SSD_CONV_DIM), F32),
            pltpu.VMEM((SUBLANES + CHUNK, SSD_DI), F32),
        ],
        compiler_params=_cparams(("arbitrary",), VMEM_LIMIT_BYTES),
        name="ssd_core",
    )(proj, proj, proj, proj, proj, conv_w, conv_b.reshape(1, SSD_CONV_DIM), dtb, aneg, dskip,
      gnorm.reshape(1, SSD_DI))


def _ssd_layer(h, norm_g, w_in, conv_w, conv_b, dt_bias, a_log, d_skip, gnorm, w_out, final_g=None):
    w = jnp.pad(w_in, ((0, 0), (0, SSD_IN_PAD - SSD_IN))).astype(BF16)
    proj = _norm_matmul(h, norm_g, w, tm=512, tn=SSD_IN_PAD // 7)
    y = _ssd_core(proj, conv_w, conv_b, dt_bias, a_log, d_skip, gnorm)
    return _out_proj(y, w_out.astype(BF16), h, final_g)


def _ret_kernel(q_ref, k_ref, v_ref, g_ref, cos_ref, sin_ref, gn_ref, y_ref, state_ref):
    @pl.when(pl.program_id(0) == 0)
    def _():
        state_ref[...] = jnp.zeros_like(state_ref)

    cos = cos_ref[...]
    sin = sin_ref[...]
    half = RET_DK // 2
    rowf = lax.broadcasted_iota(I32, (CHUNK, LANES), 0).astype(F32)
    row = lax.broadcasted_iota(I32, (CHUNK, CHUNK), 0)
    col = lax.broadcasted_iota(I32, (CHUNK, CHUNK), 1)
    diff = (row - col).astype(F32)

    def rope(ref, h, scale):
        x1 = ref[:, h * RET_DK:h * RET_DK + half]
        x2 = ref[:, h * RET_DK + half:(h + 1) * RET_DK]
        return (x1 * cos - x2 * sin) * scale, (x2 * cos + x1 * sin) * scale

    for h in range(RET_HEADS):
        log_g = float(np.log1p(-np.exp2(np.float32(-5.0 - h))))
        q1, q2 = rope(q_ref, h, 1.0)
        k1, k2 = rope(k_ref, h, RET_DK ** -0.5)
        qr = jnp.concatenate([q1, q2], axis=1).astype(BF16)
        kr = jnp.concatenate([k1, k2], axis=1).astype(BF16)
        v = v_ref[:, h * RET_DV:(h + 1) * RET_DV].astype(BF16)
        intra = jnp.where(diff >= 0, jnp.exp(jnp.maximum(diff, 0.0) * log_g), 0.0)
        scores = (_dot_nt(qr, kr) * intra).astype(BF16)
        o = _dot(scores, v)
        q_decay = jnp.exp((rowf + 1.0) * log_g)
        q_dec = jnp.concatenate([q1 * q_decay, q2 * q_decay], axis=1).astype(BF16)
        prev = state_ref[h]
        o = o + _dot(q_dec, prev.astype(BF16))
        k_decay = jnp.exp((CHUNK - 1.0 - rowf) * log_g)
        k_dec = jnp.concatenate([k1 * k_decay, k2 * k_decay], axis=1).astype(BF16)
        state_ref[h] = math.exp(CHUNK * log_g) * prev + _dot_tn(k_dec, v)
        ms = jnp.mean(o * o, axis=-1, keepdims=True)
        on = o * lax.rsqrt(ms + NORM_EPS) * gn_ref[:, h * RET_DV:(h + 1) * RET_DV]
        y_ref[:, h * RET_DV:(h + 1) * RET_DV] = (on * _silu(g_ref[:, h * RET_DV:(h + 1) * RET_DV])).astype(BF16)


def _ret_core(proj, cos, sin, gnorm):
    L = proj.shape[0]
    qk = RET_HEADS * RET_DK
    return pl.pallas_call(
        _ret_kernel,
        grid=(L // CHUNK,),
        in_specs=[
            pl.BlockSpec((CHUNK, qk), lambda c: (c, 0)),
            pl.BlockSpec((CHUNK, qk), lambda c: (c, 1)),
            pl.BlockSpec((CHUNK, RET_DVTOT), lambda c: (c, 1)),
            pl.BlockSpec((CHUNK, RET_DVTOT), lambda c: (c, 2)),
            pl.BlockSpec((CHUNK, LANES), lambda c: (c, 0)),
            pl.BlockSpec((CHUNK, LANES), lambda c: (c, 0)),
            pl.BlockSpec((1, RET_DVTOT), lambda c: (0, 0)),
        ],
        out_specs=pl.BlockSpec((CHUNK, RET_DVTOT), lambda c: (c, 0)),
        out_shape=jax.ShapeDtypeStruct((L, RET_DVTOT), BF16),
        scratch_shapes=[pltpu.VMEM((RET_HEADS, RET_DK, RET_DV), F32)],
        compiler_params=_cparams(("arbitrary",), VMEM_LIMIT_BYTES),
        name="retention_core",
    )(proj, proj, proj, proj, cos, sin, gnorm.reshape(1, RET_DVTOT))


def _ret_layer(h, norm_g, w_in, gnorm, w_out, cos, sin):
    proj = _norm_matmul(h, norm_g, w_in.astype(BF16), tm=512, tn=1024)
    y = _ret_core(proj, cos, sin, gnorm)
    return _out_proj(y, w_out.astype(BF16), h)


def _dsa_rope(x, c, s1, s2):
    half = DSA_HEADDIM // ROPE_FRACTION // 2
    parts = []
    for b in range(x.shape[1] // LANES):
        xb = x[:, b * LANES:(b + 1) * LANES]
        parts.append(xb * c + pltpu.roll(xb, LANES - half, 1) * s1 + pltpu.roll(xb, half, 1) * s2)
    return parts[0] if len(parts) == 1 else jnp.concatenate(parts, axis=1)


def _dsa_prep_kernel(q_ref, qi_ref, kv_ref, kw_ref, c_ref, s1_ref, s2_ref, kn_ref,
                     qo_ref, qio_ref, kd_ref, vo_ref, kid_ref, wo_ref):
    c = c_ref[...]
    s1 = s1_ref[...]
    s2 = s2_ref[...]
    rows = c.shape[0]
    lane = lax.broadcasted_iota(I32, (rows, LANES), 1)
    lo_half = lane < DSA_HEADDIM
    qo_ref[...] = (_dsa_rope(q_ref[...], c, s1, s2) * DSA_HEADDIM ** -0.5).astype(BF16)
    qio_ref[...] = (_dsa_rope(qi_ref[...], c, s1, s2) * IDX_DIM ** -0.5).astype(BF16)
    kwidth = DSA_KV_HEADS * DSA_HEADDIM
    kr = _dsa_rope(kv_ref[:, 0:kwidth], c, s1, s2)
    for n in range(DSA_KV_HEADS):
        xb = kr[:, (n // 2) * LANES:(n // 2 + 1) * LANES]
        sw = pltpu.roll(xb, DSA_HEADDIM, 1)
        dup = jnp.where(lo_half, xb, sw) if n % 2 == 0 else jnp.where(lo_half, sw, xb)
        kd_ref[n] = dup.astype(BF16)
    vo_ref[...] = kv_ref[:, kwidth:2 * kwidth].astype(BF16)
    kw = kw_ref[...]
    ki = jnp.where(lo_half, kw, 0.0)
    ms = jnp.sum(ki * ki, axis=-1, keepdims=True) * (1.0 / IDX_DIM)
    kin = ki * lax.rsqrt(ms + NORM_EPS) * kn_ref[...]
    kir = jnp.where(lo_half, _dsa_rope(kin, c, s1, s2), 0.0)
    kid_ref[...] = (kir + pltpu.roll(kir, DSA_HEADDIM, 1)).astype(BF16)
    wo_ref[...] = jnp.where(lane < IDX_HEADS, pltpu.roll(kw, DSA_HEADDIM, 1) * IDX_HEADS ** -0.5, 0.0)


def _dsa_prep(proj, c, s1, s2, idx_knorm, tl=512):
    L = proj.shape[0]
    kn = jnp.pad(idx_knorm.astype(F32), (0, LANES - IDX_DIM)).reshape(1, LANES)
    tok = lambda w, j: pl.BlockSpec((tl, w), lambda i: (i, j))
    return pl.pallas_call(
        _dsa_prep_kernel,
        grid=(L // tl,),
        in_specs=[
            tok(DSA_WIDTH, 0), tok(DSA_WIDTH, 1), tok(512, 6), tok(LANES, 28),
            tok(LANES, 0), tok(LANES, 0), tok(LANES, 0),
            pl.BlockSpec((1, LANES), lambda i: (0, 0)),
        ],
        out_specs=[
            tok(DSA_WIDTH, 0), tok(DSA_WIDTH, 0),
            pl.BlockSpec((DSA_KV_HEADS, tl, LANES), lambda i: (0, i, 0)),
            tok(256, 0), tok(LANES, 0), tok(LANES, 0),
        ],
        out_shape=[
            jax.ShapeDtypeStruct((L, DSA_WIDTH), BF16),
            jax.ShapeDtypeStruct((L, DSA_WIDTH), BF16),
            jax.ShapeDtypeStruct((DSA_KV_HEADS, L, LANES), BF16),
            jax.ShapeDtypeStruct((L, 256), BF16),
            jax.ShapeDtypeStruct((L, LANES), BF16),
            jax.ShapeDtypeStruct((L, LANES), F32),
        ],
        compiler_params=_cparams(("parallel",), VMEM_LIMIT_BYTES),
        name="dsa_prep",
    )(proj, proj, proj, proj, c, s1, s2, kn)


def _float_key(score):
    bits = lax.bitcast_convert_type(score, I32)
    key = jnp.where(bits < 0, bits ^ jnp.int32(0x7FFFFFFF), bits)
    return jnp.where(score == 0.0, 0, key)


def _dsa_main_kernel(q_ref, qi_ref, w_ref, g_ref, kid_ref, kd_ref, v_ref, y_ref,
                     key_ref, qis_ref, wb_ref, qs_ref, m_ref, l_ref, acc_ref, *, topk):
    i = pl.program_id(0)
    tq = DSA_QBLOCK
    tk = DSA_KTILE
    n_tiles = (i * tq + tq + tk - 1) // tk
    lane = lax.broadcasted_iota(I32, (tq, LANES), 1)
    lo_half = lane < DSA_HEADDIM

    for h in range(IDX_HEADS):
        blk = qi_ref[:, (h // 2) * LANES:(h // 2 + 1) * LANES]
        keep = lo_half if h % 2 == 0 else jnp.logical_not(lo_half)
        qis_ref[h * tq:(h + 1) * tq, :] = jnp.where(keep, blk, jnp.zeros_like(blk))
        wb_ref[h] = jnp.broadcast_to(w_ref[:, h:h + 1], (tq, LANES))
    for h in range(DSA_HEADS):
        blk = q_ref[:, (h // 2) * LANES:(h // 2 + 1) * LANES]
        keep = lo_half if h % 2 == 0 else jnp.logical_not(lo_half)
        qs_ref[h * tq:(h + 1) * tq, :] = jnp.where(keep, blk, jnp.zeros_like(blk))

    q_pos = i * tq + lax.broadcasted_iota(I32, (tq, tk), 0)
    col_in_tile = lax.broadcasted_iota(I32, (tq, tk), 1)

    def score_tile(t, carry):
        k0 = pl.multiple_of(t * tk, tk)
        s_all = _dot_nt(qis_ref[...], kid_ref[pl.ds(k0, tk), :])
        acc = jnp.zeros((tq, tk), F32)
        for h in range(IDX_HEADS):
            wb = wb_ref[h]
            wt = jnp.concatenate([wb] * (tk // LANES), axis=1)
            acc = acc + wt * jnp.maximum(s_all[h * tq:(h + 1) * tq, :], 0.0)
        key = _float_key(acc)
        key = jnp.where(k0 + col_in_tile <= q_pos, key, INT_MIN)
        key_ref[:, pl.ds(k0, tk)] = key
        return carry

    lax.fori_loop(0, n_tiles, score_tile, 0)

    def bisect(b, thr):
        cand = thr + lax.shift_left(jnp.int32(1), jnp.int32(31) - b)
        cand_b = jnp.broadcast_to(cand, (tq, LANES))

        def count_tile(t, cnt):
            k0 = pl.multiple_of(t * tk, tk)
            for s in range(tk // LANES):
                kk = key_ref[:, pl.ds(k0 + s * LANES, LANES)]
                cnt = cnt + jnp.where(kk >= cand_b, 1.0, 0.0)
            return cnt

        cnt = lax.fori_loop(0, n_tiles, count_tile, jnp.zeros((tq, LANES), F32))
        total = jnp.sum(cnt, axis=-1, keepdims=True)
        return jnp.where(total >= float(topk), cand, thr)

    thr = lax.fori_loop(0, 32, bisect, jnp.full((tq, 1), INT_MIN, I32))
    thr = jnp.maximum(thr, INT_MIN + 1)
    thr_b = jnp.broadcast_to(thr, (tq, tk))

    rows = DSA_GQA * tq
    m_ref[...] = jnp.full(m_ref.shape, M_INIT, F32)
    l_ref[...] = jnp.zeros(l_ref.shape, F32)
    acc_ref[...] = jnp.zeros(acc_ref.shape, F32)

    def attend(t, carry):
        k0 = pl.multiple_of(t * tk, tk)
        bias = jnp.where(key_ref[:, pl.ds(k0, tk)] >= thr_b, 0.0, NEG_BIG)
        bias4 = jnp.concatenate([bias] * DSA_GQA, axis=0)
        for n in range(DSA_KV_HEADS):
            s = _dot_nt(qs_ref[n * rows:(n + 1) * rows, :], kd_ref[n, pl.ds(k0, tk), :]) + bias4
            blocks = [s[:, b * LANES:(b + 1) * LANES] for b in range(tk // LANES)]
            m_old = m_ref[n]
            m_tile = functools.reduce(jnp.maximum, blocks)
            m_new = jnp.maximum(m_old, jnp.max(m_tile, axis=-1, keepdims=True))
            ps = [jnp.exp(blk - m_new) for blk in blocks]
            alpha = jnp.exp(m_old - m_new)
            l_ref[n] = alpha * l_ref[n] + functools.reduce(jnp.add, ps)
            vt = v_ref[pl.ds(k0, tk), (n // 2) * LANES:(n // 2 + 1) * LANES]
            p = jnp.concatenate(ps, axis=1).astype(BF16)
            acc_ref[n] = alpha * acc_ref[n] + _dot(p, vt)
            m_ref[n] = m_new
        return carry

    lax.fori_loop(0, n_tiles, attend, 0)
    for n in range(DSA_KV_HEADS):
        inv_l = 1.0 / jnp.sum(l_ref[n], axis=-1, keepdims=True)
        o = acc_ref[n] * inv_l
        for pr in range(DSA_GQA // 2):
            h0 = n * DSA_GQA + 2 * pr
            a = o[(2 * pr) * tq:(2 * pr + 1) * tq, :]
            b = o[(2 * pr + 1) * tq:(2 * pr + 2) * tq, :]
            if n % 2 == 0:
                blk = jnp.where(lo_half, a, pltpu.roll(b, DSA_HEADDIM, 1))
            else:
                blk = jnp.where(lo_half, pltpu.roll(a, DSA_HEADDIM, 1), b)
            cols = slice((h0 // 2) * LANES, (h0 // 2 + 1) * LANES)
            y_ref[:, cols] = (blk * _silu(g_ref[:, cols])).astype(BF16)


def _dsa_main(proj, qr, qir, kdup, vb, kidup, wi, topk):
    L = proj.shape[0]
    tq = DSA_QBLOCK
    resident = lambda shape: pl.BlockSpec(shape, lambda i: (0,) * len(shape), pipeline_mode=pl.Buffered(1))
    return pl.pallas_call(
        functools.partial(_dsa_main_kernel, topk=topk),
        grid=(L // tq,),
        in_specs=[
            pl.BlockSpec((tq, DSA_WIDTH), lambda i: (i, 0)),
            pl.BlockSpec((tq, DSA_WIDTH), lambda i: (i, 0)),
            pl.BlockSpec((tq, LANES), lambda i: (i, 0)),
            pl.BlockSpec((tq, DSA_WIDTH), lambda i: (i, 2)),
            resident((L, LANES)),
            resident((DSA_KV_HEADS, L, LANES)),
            resident((L, 256)),
        ],
        out_specs=pl.BlockSpec((tq, DSA_WIDTH), lambda i: (i, 0)),
        out_shape=jax.ShapeDtypeStruct((L, DSA_WIDTH), BF16),
        scratch_shapes=[
            pltpu.VMEM((tq, L), I32),
            pltpu.VMEM((IDX_HEADS * tq, LANES), BF16),
            pltpu.VMEM((IDX_HEADS, tq, LANES), F32),
            pltpu.VMEM((DSA_HEADS * tq, LANES), BF16),
            pltpu.VMEM((DSA_KV_HEADS, DSA_GQA * tq, LANES), F32),
            pltpu.VMEM((DSA_KV_HEADS, DSA_GQA * tq, LANES), F32),
            pltpu.VMEM((DSA_KV_HEADS, DSA_GQA * tq, LANES), F32),
        ],
        compiler_params=_cparams(("arbitrary",), VMEM_LIMIT_BYTES),
        name="dsa_index_attend",
    )(qr, qir, wi, proj, kidup, kdup, vb)


def _dsa_layer(h, norm_g, w_in, idx_knorm, w_out, c, s1, s2):
    L = h.shape[0]
    q, k, v, g, qi, ki, wi = jnp.split(w_in, np.cumsum([1024, 256, 256, 1024, 1024, 64, 16])[:-1].tolist(), axis=1)
    w = jnp.concatenate([q, qi, g, k, v, ki, wi], axis=1)
    w = jnp.pad(w, ((0, 0), (0, DSA_IN_PAD - w.shape[1]))).astype(BF16)
    proj = _norm_matmul(h, norm_g, w, tm=512, tn=768)
    qr, qir, kdup, vb, kidup, wsc = _dsa_prep(proj, c, s1, s2, idx_knorm)
    y = _dsa_main(proj, qr, qir, kdup, vb, kidup, wsc, topk=min(TOPK_MAX, L // 4))
    return _out_proj(y, w_out.astype(BF16), h)


def kernel(x, positions, l0_norm, l0_w_in, l0_conv_w, l0_conv_b, l0_dt_bias, l0_a_log, l0_d_skip, l0_gnorm, l0_w_out, l1_norm, l1_w_in, l1_gnorm, l1_w_out, l2_norm, l2_w_in, l2_idx_knorm, l2_w_out, l3_norm, l3_w_in, l3_conv_w, l3_conv_b, l3_dt_bias, l3_a_log, l3_d_skip, l3_gnorm, l3_w_out, final_norm):
    b, L, d = x.shape
    outs = []
    for bi in range(b):
        h = x[bi]
        rc, rs, dc, ds1, ds2 = _rope_tables(positions[bi].astype(F32))
        h = _ssd_layer(h, l0_norm, l0_w_in, l0_conv_w, l0_conv_b, l0_dt_bias, l0_a_log, l0_d_skip, l0_gnorm, l0_w_out)
        h = _ret_layer(h, l1_norm, l1_w_in, l1_gnorm, l1_w_out, rc, rs)
        h = _dsa_layer(h, l2_norm, l2_w_in, l2_idx_knorm, l2_w_out, dc, ds1, ds2)
        h = _ssd_layer(h, l3_norm, l3_w_in, l3_conv_w, l3_conv_b, l3_dt_bias, l3_a_log, l3_d_skip, l3_gnorm, l3_w_out,
                       final_g=final_norm)
        outs.append(h)
    return jnp.stack(outs, axis=0)
```

```python
import functools
import math

import numpy as np
import jax
import jax.numpy as jnp
from jax import lax
from jax.experimental import pallas as pl
from jax.experimental.pallas import tpu as pltpu

F32 = jnp.float32
BF16 = jnp.bfloat16
I32 = jnp.int32

LANES = 128
SUBLANES = 8
VMEM_LIMIT_BYTES = 56 * 1024 * 1024

D_MODEL = 1024
NORM_EPS = 1e-6
CHUNK = 128

SSD_DI = 2048
SSD_HEADDIM = 64
SSD_HEADS = 32
SSD_GROUPS = 8
SSD_STATE = 128
SSD_CONV = 4
SSD_CONV_DIM = 4096

RET_HEADS = 4
RET_DK = 256
RET_DV = 512
RET_DVTOT = 2048
RET_THETA = 10000.0
RET_IN = 2 * RET_HEADS * RET_DK + 2 * RET_DVTOT

DSA_HEADS = 16
DSA_HEADDIM = 64
DSA_KV_HEADS = 4
DSA_GQA = 4
DSA_WIDTH = 1024
IDX_HEADS = 16
IDX_DIM = 64
TOPK_MAX = 256
ROPE_THETA = 500000.0
ROPE_FRACTION = 4
DSA_IN_MAIN = 3584
IN_PROJ_TM = 1024
DSA_QBLOCK = 128
DSA_KTILE = 512
IDX_HCHUNK = 4
LOG2E = 1.4426950408889634
F32_TINY = 1.1754943508222875e-38
MID_BASE = 0x0080
MID_ABOVE = 0x7F00

INT_MIN = -(2 ** 31)
NEG_BIG = -1e30
M_INIT = -1e29


def _cparams(sem, vmem=None):
    return pltpu.CompilerParams(dimension_semantics=sem, vmem_limit_bytes=vmem)


def _sigmoid(x):
    return 1.0 / (1.0 + jnp.exp(-x))


def _silu(x):
    return x * _sigmoid(x)


def _dot(a, b):
    return jnp.dot(a, b, preferred_element_type=F32)


def _dot_nt(a, b):
    return lax.dot_general(a, b, (((1,), (1,)), ((), ())), preferred_element_type=F32)


def _dot_tn(a, b):
    return lax.dot_general(a, b, (((0,), (0,)), ((), ())), preferred_element_type=F32)


def _norm_matmul_kernel(h_ref, g_ref, w_ref, *rest, has_tail):
    if has_tail:
        wt_ref, o_ref, t_ref, hn_ref = rest
    else:
        o_ref, hn_ref = rest

    @pl.when(pl.program_id(1) == 0)
    def _():
        x = h_ref[...]
        ms = jnp.mean(x * x, axis=-1, keepdims=True)
        hn_ref[...] = (x * lax.rsqrt(ms + NORM_EPS) * g_ref[...]).astype(BF16)
        if has_tail:
            t_ref[...] = _dot(hn_ref[...], wt_ref[...])

    o_ref[...] = _dot(hn_ref[...], w_ref[...]).astype(BF16)


def _norm_matmul(h, g, w, n_main, tn, tm=IN_PROJ_TM):
    L, d = h.shape
    w_main = w[:, :n_main].astype(BF16)
    n_tail = w.shape[1] - n_main
    in_specs = [
        pl.BlockSpec((tm, d), lambda i, j: (i, 0)),
        pl.BlockSpec((1, d), lambda i, j: (0, 0)),
        pl.BlockSpec((d, tn), lambda i, j: (0, j)),
    ]
    out_specs = [pl.BlockSpec((tm, tn), lambda i, j: (i, j))]
    out_shape = [jax.ShapeDtypeStruct((L, n_main), BF16)]
    args = [h, g.reshape(1, d), w_main]
    if n_tail:
        in_specs.append(pl.BlockSpec((d, LANES), lambda i, j: (0, 0)))
        out_specs.append(pl.BlockSpec((tm, LANES), lambda i, j: (i, 0)))
        out_shape.append(jax.ShapeDtypeStruct((L, LANES), F32))
        args.append(jnp.pad(w[:, n_main:], ((0, 0), (0, LANES - n_tail))).astype(BF16))
    outs = pl.pallas_call(
        functools.partial(_norm_matmul_kernel, has_tail=bool(n_tail)),
        grid=(L // tm, n_main // tn),
        in_specs=in_specs,
        out_specs=out_specs,
        out_shape=out_shape,
        scratch_shapes=[pltpu.VMEM((tm, d), BF16)],
        compiler_params=_cparams(("parallel", "arbitrary"), VMEM_LIMIT_BYTES),
        name="norm_in_proj",
    )(*args)
    return outs if n_tail else (outs[0], None)


def _out_proj_kernel(y_ref, w_ref, h_ref, o_ref):
    o_ref[...] = h_ref[...] + _dot(y_ref[...], w_ref[...])


def _out_proj_final_kernel(y_ref, w_ref, h_ref, g_ref, o_ref):
    x = h_ref[...] + _dot(y_ref[...], w_ref[...])
    ms = jnp.mean(x * x, axis=-1, keepdims=True)
    o_ref[...] = x * lax.rsqrt(ms + NORM_EPS) * g_ref[...]


def _out_proj(y_bf16, w_bf16, h, final_g=None, tm=512):
    L, k = y_bf16.shape
    d = w_bf16.shape[1]
    in_specs = [
        pl.BlockSpec((tm, k), lambda i: (i, 0)),
        pl.BlockSpec((k, d), lambda i: (0, 0)),
        pl.BlockSpec((tm, d), lambda i: (i, 0)),
    ]
    args = [y_bf16, w_bf16, h]
    body = _out_proj_kernel
    if final_g is not None:
        in_specs.append(pl.BlockSpec((1, d), lambda i: (0, 0)))
        args.append(final_g.reshape(1, d))
        body = _out_proj_final_kernel
    return pl.pallas_call(
        body,
        grid=(L // tm,),
        in_specs=in_specs,
        out_specs=pl.BlockSpec((tm, d), lambda i: (i, 0)),
        out_shape=jax.ShapeDtypeStruct((L, d), F32),
        compiler_params=_cparams(("parallel",), VMEM_LIMIT_BYTES),
        name="out_proj_residual",
    )(*args)


def _rope_table_kernel(pos_ref, inv_ret_ref, inv_dsa_ref, rc_ref, rs_ref, dc_ref, ds1_ref, ds2_ref):
    pos = pos_ref[...]
    ang = pos * inv_ret_ref[...]
    rc_ref[...] = jnp.cos(ang)
    rs_ref[...] = jnp.sin(ang)
    ang_d = pos * inv_dsa_ref[...]
    lane = lax.broadcasted_iota(I32, ang_d.shape, 1) % DSA_HEADDIM
    half = DSA_HEADDIM // ROPE_FRACTION // 2
    c = jnp.cos(ang_d)
    s = jnp.sin(ang_d)
    dc_ref[...] = jnp.where(lane < 2 * half, c, 1.0)
    ds1_ref[...] = jnp.where(lane < half, -s, 0.0)
    ds2_ref[...] = jnp.where((lane >= half) & (lane < 2 * half), s, 0.0)


def _rope_tables(positions_f32, tl=1024):
    L = positions_f32.shape[0]
    inv_ret = 1.0 / (jnp.float32(RET_THETA) ** jnp.linspace(0.0, 1.0, RET_DK // 2, dtype=F32))
    r = DSA_HEADDIM // ROPE_FRACTION
    inv_dsa = jnp.float32(ROPE_THETA) ** (-jnp.arange(0, r, 2, dtype=F32) / r)
    inv_dsa_lanes = jnp.tile(inv_dsa, LANES // inv_dsa.shape[0])
    tab = jax.ShapeDtypeStruct((L, LANES), F32)
    spec = pl.BlockSpec((tl, LANES), lambda i: (i, 0))
    cst = pl.BlockSpec((1, LANES), lambda i: (0, 0))
    return pl.pallas_call(
        _rope_table_kernel,
        grid=(L // tl,),
        in_specs=[pl.BlockSpec((tl, 1), lambda i: (i, 0)), cst, cst],
        out_specs=[spec] * 5,
        out_shape=[tab] * 5,
        compiler_params=_cparams(("parallel",)),
        name="rope_tables",
    )(positions_f32.reshape(L, 1), inv_ret.reshape(1, LANES), inv_dsa_lanes.reshape(1, LANES))


def _pair_bcast(m, j):
    rows = m.shape[0]
    lane = lax.broadcasted_iota(I32, (rows, LANES), 1)
    lo = jnp.broadcast_to(m[:, 2 * j:2 * j + 1], (rows, LANES))
    hi = jnp.broadcast_to(m[:, 2 * j + 1:2 * j + 2], (rows, LANES))
    return jnp.where(lane < SSD_HEADDIM, lo, hi)


def _ssd_kernel(z_ref, x_ref, b_ref, c_ref, dt_ref, cw_ref, cb_ref, dtb_ref, aneg_ref, dskip_ref, gn_ref,
                y_ref, state_ref, tail_ref):
    @pl.when(pl.program_id(0) == 0)
    def _():
        state_ref[...] = jnp.zeros_like(state_ref)
        tail_ref[...] = jnp.zeros_like(tail_ref)

    row8 = lax.broadcasted_iota(I32, (SUBLANES, LANES), 0)

    def conv_silu(cur_ref, col0, width):
        cur = cur_ref[...].astype(F32)
        prev = tail_ref[:, col0:col0 + width]
        acc = cb_ref[:, col0:col0 + width] + cw_ref[SSD_CONV - 1:SSD_CONV, col0:col0 + width] * cur
        for j in range(SSD_CONV - 1):
            shift = SSD_CONV - 1 - j
            rolled = pltpu.roll(cur, shift, 0)
            head = jnp.where(jnp.concatenate([row8] * (width // LANES), axis=1) < shift,
                             pltpu.roll(prev, shift, 0), rolled[0:SUBLANES, :])
            shifted = jnp.concatenate([head, rolled[SUBLANES:, :]], axis=0)
            acc = acc + cw_ref[j:j + 1, col0:col0 + width] * shifted
        tail_ref[:, col0:col0 + width] = cur[CHUNK - SUBLANES:CHUNK, :]
        return _silu(acc)

    xs = conv_silu(x_ref, 0, SSD_DI)
    bm = conv_silu(b_ref, SSD_DI, SSD_GROUPS * SSD_STATE)
    cm = conv_silu(c_ref, SSD_DI + SSD_GROUPS * SSD_STATE, SSD_GROUPS * SSD_STATE)

    raw = dt_ref[...] + dtb_ref[...]
    dt = jnp.maximum(raw, 0.0) + jnp.log1p(jnp.exp(-jnp.abs(raw)))
    a = dt * aneg_ref[...]
    row = lax.broadcasted_iota(I32, (CHUNK, CHUNK), 0)
    col = lax.broadcasted_iota(I32, (CHUNK, CHUNK), 1)
    causal = row >= col
    a_cum = jnp.dot(causal.astype(F32), a, precision=lax.Precision.HIGHEST, preferred_element_type=F32)
    a_cum_t = a_cum.T
    a_last = a_cum[CHUNK - 1:CHUNK, :]
    dt_to_end = dt * jnp.exp(a_last - a_cum)
    e_acum = jnp.exp(a_cum)
    chunk_decay = jnp.exp(a_last)

    lane = lax.broadcasted_iota(I32, (CHUNK, LANES), 1)
    lo_half = lane < SSD_HEADDIM
    hpg = SSD_HEADS // SSD_GROUPS
    gw = hpg * SSD_HEADDIM
    for g in range(SSD_GROUPS):
        bg = bm[:, g * SSD_STATE:(g + 1) * SSD_STATE].astype(BF16)
        cg = cm[:, g * SSD_STATE:(g + 1) * SSD_STATE].astype(BF16)
        cb = _dot_nt(cg, bg)
        prev = state_ref[:, g * gw:(g + 1) * gw]
        y_off = _dot(cg, prev.astype(BF16))
        y_parts = []
        xdec_parts = []
        dec_parts = []
        for p in range(2):
            j = 2 * g + p
            xj = xs[:, j * LANES:(j + 1) * LANES]
            x_dt = (xj * _pair_bcast(dt, j)).astype(BF16)
            xdec_parts.append((xj * _pair_bcast(dt_to_end, j)).astype(BF16))
            yd = []
            for e in range(2):
                h = 2 * j + e
                diff = jnp.broadcast_to(a_cum[:, h:h + 1], (CHUNK, CHUNK)) - a_cum_t[h:h + 1, :]
                lmat = (cb * jnp.exp(jnp.where(causal, diff, -jnp.inf))).astype(BF16)
                yd.append(_dot(lmat, x_dt))
            y_diag = jnp.where(lo_half, yd[0], yd[1])
            y_parts.append(y_diag + y_off[:, p * LANES:(p + 1) * LANES] * _pair_bcast(e_acum, j)
                           + dskip_ref[:, j * LANES:(j + 1) * LANES] * xj)
            dec_parts.append(_pair_bcast(chunk_decay, j))
        xdec = jnp.concatenate(xdec_parts, axis=1)
        new_state = _dot_tn(bg, xdec)
        state_ref[:, g * gw:(g + 1) * gw] = jnp.concatenate(dec_parts, axis=1) * prev + new_state
        yg = jnp.concatenate(y_parts, axis=1) * _silu(z_ref[:, g * gw:(g + 1) * gw].astype(F32))
        ms = jnp.mean(yg * yg, axis=-1, keepdims=True)
        y_ref[:, g * gw:(g + 1) * gw] = (yg * lax.rsqrt(ms + NORM_EPS) * gn_ref[:, g * gw:(g + 1) * gw]).astype(BF16)


def _ssd_core(proj, dt_raw, conv_w, conv_b, dt_bias, a_log, d_skip, gnorm):
    L = proj.shape[0]
    nb = SSD_GROUPS * SSD_STATE
    pad = LANES - SSD_HEADS
    dtb = jnp.pad(dt_bias, (0, pad)).reshape(1, LANES)
    aneg = jnp.pad(-jnp.exp(a_log.astype(F32)), (0, pad)).reshape(1, LANES)
    dskip = jnp.repeat(d_skip.astype(F32), SSD_HEADDIM).reshape(1, SSD_DI)
    full = lambda shape: pl.BlockSpec(shape, lambda c: (0, 0))
    return pl.pallas_call(
        _ssd_kernel,
        grid=(L // CHUNK,),
        in_specs=[
            pl.BlockSpec((CHUNK, SSD_DI), lambda c: (c, 0)),
            pl.BlockSpec((CHUNK, SSD_DI), lambda c: (c, 1)),
            pl.BlockSpec((CHUNK, nb), lambda c: (c, 2 * SSD_DI // nb)),
            pl.BlockSpec((CHUNK, nb), lambda c: (c, 2 * SSD_DI // nb + 1)),
            pl.BlockSpec((CHUNK, LANES), lambda c: (c, 0)),
            full((SSD_CONV, SSD_CONV_DIM)), full((1, SSD_CONV_DIM)), full((1, LANES)), full((1, LANES)),
            full((1, SSD_DI)), full((1, SSD_DI)),
        ],
        out_specs=pl.BlockSpec((CHUNK, SSD_DI), lambda c: (c, 0)),
        out_shape=jax.ShapeDtypeStruct((L, SSD_DI), BF16),
        scratch_shapes=[
            pltpu.VMEM((SSD_STATE, SSD_DI), F32),
            pltpu.VMEM((SUBLANES, SSD_CONV_DIM), F32),
        ],
        compiler_params=_cparams(("arbitrary",), VMEM_LIMIT_BYTES),
        name="ssd_core",
    )(proj, proj, proj, proj, dt_raw, conv_w, conv_b.reshape(1, SSD_CONV_DIM), dtb, aneg, dskip,
      gnorm.reshape(1, SSD_DI))


def _ssd_layer(h, norm_g, w_in, conv_w, conv_b, dt_bias, a_log, d_skip, gnorm, w_out, final_g=None):
    proj, dt_raw = _norm_matmul(h, norm_g, w_in, SSD_DI + SSD_CONV_DIM, tn=1024)
    y = _ssd_core(proj, dt_raw, conv_w, conv_b, dt_bias, a_log, d_skip, gnorm)
    return _out_proj(y, w_out.astype(BF16), h, final_g)


def _ret_kernel(q_ref, k_ref, v_ref, g_ref, cos_ref, sin_ref, gn_ref, y_ref, state_ref):
    @pl.when(pl.program_id(0) == 0)
    def _():
        state_ref[...] = jnp.zeros_like(state_ref)

    cos = cos_ref[...]
    sin = sin_ref[...]
    half = RET_DK // 2
    rowf = lax.broadcasted_iota(I32, (CHUNK, LANES), 0).astype(F32)
    row = lax.broadcasted_iota(I32, (CHUNK, CHUNK), 0)
    col = lax.broadcasted_iota(I32, (CHUNK, CHUNK), 1)
    diff = (row - col).astype(F32)

    def rope(ref, h, scale):
        x1 = ref[:, h * RET_DK:h * RET_DK + half].astype(F32)
        x2 = ref[:, h * RET_DK + half:(h + 1) * RET_DK].astype(F32)
        return (x1 * cos - x2 * sin) * scale, (x2 * cos + x1 * sin) * scale

    for h in range(RET_HEADS):
        log_g = float(np.log1p(-np.exp2(np.float32(-5.0 - h))))
        q1, q2 = rope(q_ref, h, 1.0)
        k1, k2 = rope(k_ref, h, RET_DK ** -0.5)
        qr = jnp.concatenate([q1, q2], axis=1).astype(BF16)
        kr = jnp.concatenate([k1, k2], axis=1).astype(BF16)
        v = v_ref[:, h * RET_DV:(h + 1) * RET_DV].astype(BF16)
        intra = jnp.where(diff >= 0, jnp.exp(jnp.maximum(diff, 0.0) * log_g), 0.0)
        scores = (_dot_nt(qr, kr) * intra).astype(BF16)
        o = _dot(scores, v)
        q_decay = jnp.exp((rowf + 1.0) * log_g)
        q_dec = jnp.concatenate([q1 * q_decay, q2 * q_decay], axis=1).astype(BF16)
        prev = state_ref[h]
        o = o + _dot(q_dec, prev.astype(BF16))
        k_decay = jnp.exp((CHUNK - 1.0 - rowf) * log_g)
        k_dec = jnp.concatenate([k1 * k_decay, k2 * k_decay], axis=1).astype(BF16)
        state_ref[h] = math.exp(CHUNK * log_g) * prev + _dot_tn(k_dec, v)
        ms = jnp.mean(o * o, axis=-1, keepdims=True)
        on = o * lax.rsqrt(ms + NORM_EPS) * gn_ref[:, h * RET_DV:(h + 1) * RET_DV]
        y_ref[:, h * RET_DV:(h + 1) * RET_DV] = (on * _silu(g_ref[:, h * RET_DV:(h + 1) * RET_DV].astype(F32))).astype(BF16)


def _ret_core(proj, cos, sin, gnorm):
    L = proj.shape[0]
    qk = RET_HEADS * RET_DK
    return pl.pallas_call(
        _ret_kernel,
        grid=(L // CHUNK,),
        in_specs=[
            pl.BlockSpec((CHUNK, qk), lambda c: (c, 0)),
            pl.BlockSpec((CHUNK, qk), lambda c: (c, 1)),
            pl.BlockSpec((CHUNK, RET_DVTOT), lambda c: (c, 1)),
            pl.BlockSpec((CHUNK, RET_DVTOT), lambda c: (c, 2)),
            pl.BlockSpec((CHUNK, LANES), lambda c: (c, 0)),
            pl.BlockSpec((CHUNK, LANES), lambda c: (c, 0)),
            pl.BlockSpec((1, RET_DVTOT), lambda c: (0, 0)),
        ],
        out_specs=pl.BlockSpec((CHUNK, RET_DVTOT), lambda c: (c, 0)),
        out_shape=jax.ShapeDtypeStruct((L, RET_DVTOT), BF16),
        scratch_shapes=[pltpu.VMEM((RET_HEADS, RET_DK, RET_DV), F32)],
        compiler_params=_cparams(("arbitrary",), VMEM_LIMIT_BYTES),
        name="retention_core",
    )(proj, proj, proj, proj, cos, sin, gnorm.reshape(1, RET_DVTOT))


def _ret_layer(h, norm_g, w_in, gnorm, w_out, cos, sin):
    proj, _ = _norm_matmul(h, norm_g, w_in, RET_IN, tn=1024)
    y = _ret_core(proj, cos, sin, gnorm)
    return _out_proj(y, w_out.astype(BF16), h)


def _dsa_rope(x, c, s1, s2):
    half = DSA_HEADDIM // ROPE_FRACTION // 2
    parts = []
    for b in range(x.shape[1] // LANES):
        xb = x[:, b * LANES:(b + 1) * LANES]
        parts.append(xb * c + pltpu.roll(xb, LANES - half, 1) * s1 + pltpu.roll(xb, half, 1) * s2)
    return parts[0] if len(parts) == 1 else jnp.concatenate(parts, axis=1)


def _dsa_prep_kernel(q_ref, qi_ref, kv_ref, kw_ref, c_ref, s1_ref, s2_ref, kn_ref,
                     qx_ref, qix_ref, ko_ref, v1_ref, kid_ref, wo_ref):
    c = c_ref[...]
    s1 = s1_ref[...]
    s2 = s2_ref[...]
    rows = c.shape[0]
    lane = lax.broadcasted_iota(I32, (rows, LANES), 1)
    lo_half = lane < DSA_HEADDIM
    hi_half = jnp.logical_not(lo_half)

    def expand(x, scale, out_ref, target_half):
        for h in range(x.shape[1] // DSA_HEADDIM):
            blk = x[:, (h // 2) * LANES:(h // 2 + 1) * LANES] * scale
            if h % 2 != target_half(h):
                blk = pltpu.roll(blk, DSA_HEADDIM, 1)
            keep = lo_half if target_half(h) == 0 else hi_half
            out_ref[:, h * LANES:(h + 1) * LANES] = jnp.where(keep, blk, 0.0).astype(BF16)

    expand(_dsa_rope(q_ref[...].astype(F32), c, s1, s2), LOG2E * DSA_HEADDIM ** -0.5, qx_ref, lambda h: (h // DSA_GQA) % 2)
    expand(_dsa_rope(qi_ref[...].astype(F32), c, s1, s2), IDX_DIM ** -0.5, qix_ref, lambda h: h % 2)
    kwidth = DSA_KV_HEADS * DSA_HEADDIM
    ko_ref[...] = _dsa_rope(kv_ref[:, 0:kwidth].astype(F32), c, s1, s2).astype(BF16)
    for n in range(DSA_KV_HEADS):
        vb = kv_ref[:, kwidth + (n // 2) * LANES:kwidth + (n // 2 + 1) * LANES].astype(F32)
        if n % 2 == 1:
            vb = pltpu.roll(vb, DSA_HEADDIM, 1)
        v1_ref[n] = jnp.where(lo_half, vb, 1.0).astype(BF16)
    kw = kw_ref[...]
    ki = jnp.where(lo_half, kw, 0.0)
    ms = jnp.sum(ki * ki, axis=-1, keepdims=True) * (1.0 / IDX_DIM)
    kin = ki * lax.rsqrt(ms + NORM_EPS) * kn_ref[...]
    kir = jnp.where(lo_half, _dsa_rope(kin, c, s1, s2), 0.0)
    kid_ref[...] = (kir + pltpu.roll(kir, DSA_HEADDIM, 1)).astype(BF16)
    wo_ref[...] = jnp.where(lane < IDX_HEADS, pltpu.roll(kw, DSA_HEADDIM, 1) * IDX_HEADS ** -0.5, 0.0)


def _dsa_prep(proj, kw, c, s1, s2, idx_knorm, tl=512):
    L = proj.shape[0]
    kn = jnp.pad(idx_knorm.astype(F32), (0, LANES - IDX_DIM)).reshape(1, LANES)
    tok = lambda w, j: pl.BlockSpec((tl, w), lambda i: (i, j))
    return pl.pallas_call(
        _dsa_prep_kernel,
        grid=(L // tl,),
        in_specs=[
            tok(DSA_WIDTH, 0), tok(DSA_WIDTH, 1), tok(512, 6), tok(LANES, 0),
            tok(LANES, 0), tok(LANES, 0), tok(LANES, 0),
            pl.BlockSpec((1, LANES), lambda i: (0, 0)),
        ],
        out_specs=[
            tok(DSA_HEADS * LANES, 0), tok(IDX_HEADS * LANES, 0), tok(256, 0),
            pl.BlockSpec((DSA_KV_HEADS, tl, LANES), lambda i: (0, i, 0)),
            tok(LANES, 0), tok(LANES, 0),
        ],
        out_shape=[
            jax.ShapeDtypeStruct((L, DSA_HEADS * LANES), BF16),
            jax.ShapeDtypeStruct((L, IDX_HEADS * LANES), BF16),
            jax.ShapeDtypeStruct((L, 256), BF16),
            jax.ShapeDtypeStruct((DSA_KV_HEADS, L, LANES), BF16),
            jax.ShapeDtypeStruct((L, LANES), BF16),
            jax.ShapeDtypeStruct((L, LANES), F32),
        ],
        compiler_params=_cparams(("parallel",), VMEM_LIMIT_BYTES),
        name="dsa_prep",
    )(proj, proj, proj, kw, c, s1, s2, kn)


def _bf16_pattern(p16):
    return lax.bitcast_convert_type(lax.shift_left(p16, 16), F32)


def _dsa_main_kernel(qx_ref, qix_ref, w_ref, g_ref, kid_ref, k_ref, v1_ref, y_ref,
                     key_ref, sb_ref, wb_ref, m_ref, acc_ref, *, topk):
    i = pl.program_id(0)
    tq = DSA_QBLOCK
    tk = DSA_KTILE
    n_tiles = (i * tq + tq + tk - 1) // tk
    nb = tk // LANES
    lane = lax.broadcasted_iota(I32, (tq, LANES), 1)
    lo_half = lane < DSA_HEADDIM
    ftopk = float(topk)

    for h in range(IDX_HEADS):
        wb_ref[h] = jnp.broadcast_to(w_ref[:, h:h + 1], (tq, LANES))

    def stacked(ref, h0, nh):
        return jnp.concatenate([ref[:, h * LANES:(h + 1) * LANES] for h in range(h0, h0 + nh)], axis=0)

    q_pos = i * tq + lax.broadcasted_iota(I32, (tq, tk), 0)
    col_in_tile = lax.broadcasted_iota(I32, (tq, tk), 1)

    def score_tile(t, carry):
        k0 = pl.multiple_of(t * tk, tk)
        kt = kid_ref[pl.ds(k0, tk), :]
        acc = jnp.zeros((tq, tk), F32)
        hc = IDX_HCHUNK
        for c in range(IDX_HEADS // hc):
            s_c = _dot_nt(stacked(qix_ref, c * hc, hc), kt)
            for hh in range(hc):
                wt = jnp.concatenate([wb_ref[c * hc + hh]] * nb, axis=1)
                acc = acc + wt * jnp.maximum(s_c[hh * tq:(hh + 1) * tq, :], 0.0)
        acc = jnp.where(jnp.abs(acc) < F32_TINY, 0.0, acc)
        bits = lax.bitcast_convert_type(acc, I32)
        key = jnp.where(bits < 0, bits ^ jnp.int32(0x7FFFFFFF), bits)
        valid = k0 + col_in_tile <= q_pos
        key_ref[:, pl.ds(k0, tk)] = jnp.where(valid, key, INT_MIN)
        trunc = lax.bitcast_convert_type(bits & jnp.int32(-65536), F32)
        sb_ref[:, pl.ds(k0, tk)] = jnp.where(valid, trunc, -jnp.inf).astype(BF16)
        return carry

    lax.fori_loop(0, n_tiles, score_tile, 0)


    def count_ge_bf16(cand_f32):
        cbt = jnp.concatenate([cand_f32.astype(BF16)] * nb, axis=1)

        def body(t, hits):
            k0 = pl.multiple_of(t * tk, tk)
            hit = jnp.where(sb_ref[:, pl.ds(k0, tk)] >= cbt, jnp.ones((), BF16), jnp.zeros((), BF16))
            return hits + functools.reduce(jnp.add, [hit[:, b * LANES:(b + 1) * LANES] for b in range(nb)])

        hits = lax.fori_loop(0, n_tiles, body, jnp.zeros((tq, LANES), BF16))
        return jnp.broadcast_to(jnp.sum(hits.astype(F32), axis=-1, keepdims=True), (tq, LANES))

    def hi_pattern(k16):
        return jnp.where(k16 >= 0, k16, k16 ^ jnp.int32(0x7FFF))

    def stage1(b, t16):
        cand = t16 + lax.shift_left(jnp.int32(1), jnp.int32(15) - b)
        cnt = count_ge_bf16(_bf16_pattern(hi_pattern(cand)))
        return jnp.where(cnt >= ftopk, cand, t16)

    t16 = lax.fori_loop(0, 16, stage1, jnp.full((tq, LANES), -32768, I32))

    t16_t = jnp.concatenate([t16] * nb, axis=1)

    def remap_tile(t, carry):
        k0 = pl.multiple_of(t * tk, tk)
        key = key_ref[:, pl.ds(k0, tk)]
        hi = lax.shift_right_arithmetic(key, 16)
        mid = (lax.shift_right_logical(key, 2) & jnp.int32(0x3FFF)) + jnp.int32(MID_BASE)
        pat = jnp.where(hi > t16_t, jnp.int32(MID_ABOVE), jnp.where(hi == t16_t, mid, 0))
        sb_ref[:, pl.ds(k0, tk)] = _bf16_pattern(pat).astype(BF16)
        return carry

    lax.fori_loop(0, n_tiles, remap_tile, 0)

    def stage2(b, u14):
        cand = u14 + lax.shift_left(jnp.int32(1), jnp.int32(13) - b)
        cnt = count_ge_bf16(_bf16_pattern(cand + jnp.int32(MID_BASE)))
        return jnp.where(cnt >= ftopk, cand, u14)

    u14 = lax.fori_loop(0, 14, stage2, jnp.zeros((tq, LANES), I32))
    thr = lax.shift_left(t16, 16) + lax.shift_left(u14, 2)

    def count_ge_key(cand):
        def body(t, cnts):
            k0 = pl.multiple_of(t * tk, tk)
            cnts = list(cnts)
            for s in range(nb):
                kk = key_ref[:, pl.ds(k0 + s * LANES, LANES)]
                cnts[s % 2] = cnts[s % 2] + jnp.where(kk >= cand, 1.0, 0.0)
            return tuple(cnts)

        z = jnp.zeros((tq, LANES), F32)
        c0, c1 = lax.fori_loop(0, n_tiles, body, (z, z))
        return jnp.broadcast_to(jnp.sum(c0 + c1, axis=-1, keepdims=True), (tq, LANES))

    for low_bit in (2, 1):
        cand = thr + low_bit
        thr = jnp.where(count_ge_key(cand) >= ftopk, cand, thr)
    thr = jnp.where(t16 <= -32768, INT_MIN + 1, jnp.maximum(thr, INT_MIN + 1))
    thr_b = jnp.concatenate([thr] * nb, axis=1)

    rows = DSA_GQA * tq
    m_ref[...] = jnp.full(m_ref.shape, M_INIT, F32)
    acc_ref[...] = jnp.zeros(acc_ref.shape, F32)

    def attend(t, carry):
        k0 = pl.multiple_of(t * tk, tk)
        bias = jnp.where(key_ref[:, pl.ds(k0, tk)] >= thr_b, 0.0, NEG_BIG)
        bias4 = jnp.concatenate([bias] * DSA_GQA, axis=0)
        for n in range(DSA_KV_HEADS):
            kt = k_ref[pl.ds(k0, tk), (n // 2) * LANES:(n // 2 + 1) * LANES]
            s = _dot_nt(stacked(qx_ref, n * DSA_GQA, DSA_GQA), kt) + bias4
            blocks = [s[:, b * LANES:(b + 1) * LANES] for b in range(nb)]
            m_old = m_ref[n]
            m_new = jnp.maximum(m_old, jnp.max(functools.reduce(jnp.maximum, blocks), axis=-1, keepdims=True))
            p = jnp.concatenate([jnp.exp2(blk - m_new) for blk in blocks], axis=1).astype(BF16)
            acc_ref[n] = jnp.exp2(m_old - m_new) * acc_ref[n] + _dot(p, v1_ref[n, pl.ds(k0, tk), :])
            m_ref[n] = m_new
        return carry

    lax.fori_loop(0, n_tiles, attend, 0)
    outs = []
    for n in range(DSA_KV_HEADS):
        acc = acc_ref[n]
        o = acc * (1.0 / pltpu.roll(acc, DSA_HEADDIM, 1))
        outs.extend(o[e * tq:(e + 1) * tq, :] for e in range(DSA_GQA))
    for j in range(DSA_HEADS // 2):
        blk = jnp.where(lo_half, outs[2 * j], pltpu.roll(outs[2 * j + 1], DSA_HEADDIM, 1))
        cols = slice(j * LANES, (j + 1) * LANES)
        y_ref[:, cols] = (blk * _silu(g_ref[:, cols].astype(F32))).astype(BF16)


def _dsa_main(proj, qx, qix, kr, v1, kidup, wi, topk):
    L = proj.shape[0]
    tq = DSA_QBLOCK
    resident = lambda shape: pl.BlockSpec(shape, lambda i: (0,) * len(shape), pipeline_mode=pl.Buffered(1))
    return pl.pallas_call(
        functools.partial(_dsa_main_kernel, topk=topk),
        grid=(L // tq,),
        in_specs=[
            pl.BlockSpec((tq, DSA_HEADS * LANES), lambda i: (i, 0)),
            pl.BlockSpec((tq, IDX_HEADS * LANES), lambda i: (i, 0)),
            pl.BlockSpec((tq, LANES), lambda i: (i, 0)),
            pl.BlockSpec((tq, DSA_WIDTH), lambda i: (i, 2)),
            resident((L, LANES)),
            resident((L, 256)),
            resident((DSA_KV_HEADS, L, LANES)),
        ],
        out_specs=pl.BlockSpec((tq, DSA_WIDTH), lambda i: (i, 0)),
        out_shape=jax.ShapeDtypeStruct((L, DSA_WIDTH), BF16),
        scratch_shapes=[
            pltpu.VMEM((tq, L), I32),
            pltpu.VMEM((tq, L), BF16),
            pltpu.VMEM((IDX_HEADS, tq, LANES), F32),
            pltpu.VMEM((DSA_KV_HEADS, DSA_GQA * tq, LANES), F32),
            pltpu.VMEM((DSA_KV_HEADS, DSA_GQA * tq, LANES), F32),
        ],
        compiler_params=_cparams(("arbitrary",), VMEM_LIMIT_BYTES),
        name="dsa_index_attend",
    )(qx, qix, wi, proj, kidup, kr, v1)


def _dsa_layer(h, norm_g, w_in, idx_knorm, w_out, c, s1, s2):
    L = h.shape[0]
    q, k, v, g, qi, ki, wi = jnp.split(w_in, np.cumsum([1024, 256, 256, 1024, 1024, 64, 16])[:-1].tolist(), axis=1)
    w = jnp.concatenate([q, qi, g, k, v, ki, wi], axis=1)
    proj, kw = _norm_matmul(h, norm_g, w, DSA_IN_MAIN, tn=DSA_IN_MAIN // 2)
    qx, qix, kr, v1, kidup, wsc = _dsa_prep(proj, kw, c, s1, s2, idx_knorm)
    y = _dsa_main(proj, qx, qix, kr, v1, kidup, wsc, topk=min(TOPK_MAX, L // 4))
    return _out_proj(y, w_out.astype(BF16), h)


def kernel(x, positions, l0_norm, l0_w_in, l0_conv_w, l0_conv_b, l0_dt_bias, l0_a_log, l0_d_skip, l0_gnorm, l0_w_out, l1_norm, l1_w_in, l1_gnorm, l1_w_out, l2_norm, l2_w_in, l2_idx_knorm, l2_w_out, l3_norm, l3_w_in, l3_conv_w, l3_conv_b, l3_dt_bias, l3_a_log, l3_d_skip, l3_gnorm, l3_w_out, final_norm):
    b, L, d = x.shape
    outs = []
    for bi in range(b):
        h = x[bi]
        rc, rs, dc, ds1, ds2 = _rope_tables(positions[bi].astype(F32))
        h = _ssd_layer(h, l0_norm, l0_w_in, l0_conv_w, l0_conv_b, l0_dt_bias, l0_a_log, l0_d_skip, l0_gnorm, l0_w_out)
        h = _ret_layer(h, l1_norm, l1_w_in, l1_gnorm, l1_w_out, rc, rs)
        h = _dsa_layer(h, l2_norm, l2_w_in, l2_idx_knorm, l2_w_out, dc, ds1, ds2)
        h = _ssd_layer(h, l3_norm, l3_w_in, l3_conv_w, l3_conv_b, l3_dt_bias, l3_a_log, l3_d_skip, l3_gnorm, l3_w_out,
                       final_g=final_norm)
        outs.append(h)
    return jnp.stack(outs, axis=0)
```

```python
import functools
import math

import numpy as np
import jax
import jax.numpy as jnp
from jax import lax
from jax.experimental import pallas as pl
from jax.experimental.pallas import tpu as pltpu

F32 = jnp.float32
BF16 = jnp.bfloat16
I32 = jnp.int32

LANES = 128
SUBLANES = 8
VMEM_LIMIT_BYTES = 56 * 1024 * 1024

D_MODEL = 1024
NORM_EPS = 1e-6
CHUNK = 128

SSD_DI = 2048
SSD_HEADDIM = 64
SSD_HEADS = 32
SSD_GROUPS = 8
SSD_STATE = 128
SSD_CONV = 4
SSD_CONV_DIM = 4096

RET_HEADS = 4
RET_DK = 256
RET_DV = 512
RET_DVTOT = 2048
RET_THETA = 10000.0
RET_IN = 2 * RET_HEADS * RET_DK + 2 * RET_DVTOT

DSA_HEADS = 16
DSA_HEADDIM = 64
DSA_KV_HEADS = 4
DSA_GQA = 4
DSA_WIDTH = 1024
IDX_HEADS = 16
IDX_DIM = 64
TOPK_MAX = 256
ROPE_THETA = 500000.0
ROPE_FRACTION = 4
DSA_IN_MAIN = 3584
IN_PROJ_TM = 2048
DSA_QBLOCK = 128
DSA_SCORE_TILE = 1024
DSA_COUNT_TILE = 1024
DSA_ATT_TILE = 1024
IDX_HCHUNK = 4
LOG2E = 1.4426950408889634
F32_TINY = 1.1754943508222875e-38
MID_BASE = 0x0080
MID_ABOVE = 0x7F00

L_MAX = 32768.0
INT_MIN = -(2 ** 31)
NEG_BIG = -1e30
M_INIT = -1e29


def _cparams(sem, vmem=None):
    return pltpu.CompilerParams(dimension_semantics=sem, vmem_limit_bytes=vmem)


def _sigmoid(x):
    return 1.0 / (1.0 + jnp.exp(-x))


def _silu(x):
    return x * _sigmoid(x)


def _dot(a, b):
    return jnp.dot(a, b, preferred_element_type=F32)


def _dot_nt(a, b):
    return lax.dot_general(a, b, (((1,), (1,)), ((), ())), preferred_element_type=F32)


def _dot_tn(a, b):
    return lax.dot_general(a, b, (((0,), (0,)), ((), ())), preferred_element_type=F32)


def _norm_matmul_kernel(h_ref, g_ref, w_ref, *rest, has_tail):
    if has_tail:
        wt_ref, o_ref, t_ref, hn_ref = rest
    else:
        o_ref, hn_ref = rest

    @pl.when(pl.program_id(1) == 0)
    def _():
        x = h_ref[...]
        ms = jnp.mean(x * x, axis=-1, keepdims=True)
        hn_ref[...] = (x * lax.rsqrt(ms + NORM_EPS) * g_ref[...]).astype(BF16)
        if has_tail:
            t_ref[...] = _dot(hn_ref[...], wt_ref[...])

    o_ref[...] = _dot(hn_ref[...], w_ref[...]).astype(BF16)


def _norm_matmul(h, g, w, n_main, tn, tm=IN_PROJ_TM):
    L, d = h.shape
    w_main = w[:, :n_main].astype(BF16)
    n_tail = w.shape[1] - n_main
    in_specs = [
        pl.BlockSpec((tm, d), lambda i, j: (i, 0)),
        pl.BlockSpec((1, d), lambda i, j: (0, 0)),
        pl.BlockSpec((d, tn), lambda i, j: (0, j)),
    ]
    out_specs = [pl.BlockSpec((tm, tn), lambda i, j: (i, j))]
    out_shape = [jax.ShapeDtypeStruct((L, n_main), BF16)]
    args = [h, g.reshape(1, d), w_main]
    if n_tail:
        in_specs.append(pl.BlockSpec((d, LANES), lambda i, j: (0, 0)))
        out_specs.append(pl.BlockSpec((tm, LANES), lambda i, j: (i, 0)))
        out_shape.append(jax.ShapeDtypeStruct((L, LANES), F32))
        args.append(jnp.pad(w[:, n_main:], ((0, 0), (0, LANES - n_tail))).astype(BF16))
    outs = pl.pallas_call(
        functools.partial(_norm_matmul_kernel, has_tail=bool(n_tail)),
        grid=(L // tm, n_main // tn),
        in_specs=in_specs,
        out_specs=out_specs,
        out_shape=out_shape,
        scratch_shapes=[pltpu.VMEM((tm, d), BF16)],
        compiler_params=_cparams(("parallel", "arbitrary"), VMEM_LIMIT_BYTES),
        name="norm_in_proj",
    )(*args)
    return outs if n_tail else (outs[0], None)


def _out_proj_kernel(y_ref, w_ref, h_ref, o_ref):
    o_ref[...] = h_ref[...] + _dot(y_ref[...], w_ref[...])


def _out_proj_final_kernel(y_ref, w_ref, h_ref, g_ref, o_ref):
    x = h_ref[...] + _dot(y_ref[...], w_ref[...])
    ms = jnp.mean(x * x, axis=-1, keepdims=True)
    o_ref[...] = x * lax.rsqrt(ms + NORM_EPS) * g_ref[...]


def _out_proj(y_bf16, w_bf16, h, final_g=None, tm=1024):
    L, k = y_bf16.shape
    d = w_bf16.shape[1]
    in_specs = [
        pl.BlockSpec((tm, k), lambda i: (i, 0)),
        pl.BlockSpec((k, d), lambda i: (0, 0)),
        pl.BlockSpec((tm, d), lambda i: (i, 0)),
    ]
    args = [y_bf16, w_bf16, h]
    body = _out_proj_kernel
    if final_g is not None:
        in_specs.append(pl.BlockSpec((1, d), lambda i: (0, 0)))
        args.append(final_g.reshape(1, d))
        body = _out_proj_final_kernel
    return pl.pallas_call(
        body,
        grid=(L // tm,),
        in_specs=in_specs,
        out_specs=pl.BlockSpec((tm, d), lambda i: (i, 0)),
        out_shape=jax.ShapeDtypeStruct((L, d), F32),
        compiler_params=_cparams(("parallel",), VMEM_LIMIT_BYTES),
        name="out_proj_residual",
    )(*args)


def _rope_table_kernel(pos_ref, inv_ret_ref, inv_dsa_ref, rc_ref, rs_ref, dc_ref, ds1_ref, ds2_ref):
    pos = pos_ref[...]
    ang = pos * inv_ret_ref[...]
    rc_ref[...] = jnp.cos(ang)
    rs_ref[...] = jnp.sin(ang)
    ang_d = pos * inv_dsa_ref[...]
    lane = lax.broadcasted_iota(I32, ang_d.shape, 1) % DSA_HEADDIM
    half = DSA_HEADDIM // ROPE_FRACTION // 2
    c = jnp.cos(ang_d)
    s = jnp.sin(ang_d)
    dc_ref[...] = jnp.where(lane < 2 * half, c, 1.0)
    ds1_ref[...] = jnp.where(lane < half, -s, 0.0)
    ds2_ref[...] = jnp.where((lane >= half) & (lane < 2 * half), s, 0.0)


def _rope_tables(positions_f32, tl=1024):
    L = positions_f32.shape[0]
    inv_ret = 1.0 / (jnp.float32(RET_THETA) ** jnp.linspace(0.0, 1.0, RET_DK // 2, dtype=F32))
    r = DSA_HEADDIM // ROPE_FRACTION
    inv_dsa = jnp.float32(ROPE_THETA) ** (-jnp.arange(0, r, 2, dtype=F32) / r)
    inv_dsa_lanes = jnp.tile(inv_dsa, LANES // inv_dsa.shape[0])
    tab = jax.ShapeDtypeStruct((L, LANES), F32)
    spec = pl.BlockSpec((tl, LANES), lambda i: (i, 0))
    cst = pl.BlockSpec((1, LANES), lambda i: (0, 0))
    return pl.pallas_call(
        _rope_table_kernel,
        grid=(L // tl,),
        in_specs=[pl.BlockSpec((tl, 1), lambda i: (i, 0)), cst, cst],
        out_specs=[spec] * 5,
        out_shape=[tab] * 5,
        compiler_params=_cparams(("parallel",)),
        name="rope_tables",
    )(positions_f32.reshape(L, 1), inv_ret.reshape(1, LANES), inv_dsa_lanes.reshape(1, LANES))


def _ssd_kernel(z_ref, x_ref, b_ref, c_ref, dt_ref, cw_ref, cb_ref, dtb_ref, aneg_ref, dskip_ref, gn_ref, ex_ref,
                y_ref, state_ref, tail_ref):
    @pl.when(pl.program_id(0) == 0)
    def _():
        state_ref[...] = jnp.zeros_like(state_ref)
        tail_ref[...] = jnp.zeros_like(tail_ref)

    row8 = lax.broadcasted_iota(I32, (SUBLANES, LANES), 0)

    def conv_silu(cur_ref, col0, width):
        cur = cur_ref[...].astype(F32)
        prev = tail_ref[:, col0:col0 + width]
        acc = cb_ref[:, col0:col0 + width] + cw_ref[SSD_CONV - 1:SSD_CONV, col0:col0 + width] * cur
        for j in range(SSD_CONV - 1):
            shift = SSD_CONV - 1 - j
            rolled = pltpu.roll(cur, shift, 0)
            head = jnp.where(jnp.concatenate([row8] * (width // LANES), axis=1) < shift,
                             pltpu.roll(prev, shift, 0), rolled[0:SUBLANES, :])
            shifted = jnp.concatenate([head, rolled[SUBLANES:, :]], axis=0)
            acc = acc + cw_ref[j:j + 1, col0:col0 + width] * shifted
        tail_ref[:, col0:col0 + width] = cur[CHUNK - SUBLANES:CHUNK, :]
        return _silu(acc)

    xs = conv_silu(x_ref, 0, SSD_DI)
    bm = conv_silu(b_ref, SSD_DI, SSD_GROUPS * SSD_STATE)
    cm = conv_silu(c_ref, SSD_DI + SSD_GROUPS * SSD_STATE, SSD_GROUPS * SSD_STATE)

    raw = dt_ref[...] + dtb_ref[...]
    dt = jnp.maximum(raw, 0.0) + jnp.log1p(jnp.exp(-jnp.abs(raw)))
    a = dt * aneg_ref[...]
    row = lax.broadcasted_iota(I32, (CHUNK, CHUNK), 0)
    col = lax.broadcasted_iota(I32, (CHUNK, CHUNK), 1)
    causal = row >= col
    a_cum = jnp.dot(causal.astype(F32), a, precision=lax.Precision.HIGHEST, preferred_element_type=F32)
    a_cum_t = a_cum.T

    def expand_heads(m, terms):
        out = None
        for _ in range(terms):
            part = m.astype(BF16)
            m = m - part.astype(F32)
            d = _dot(part, ex_ref[...])
            out = d if out is None else out + d
        return out

    dt_x = expand_heads(dt, 2)
    a_cum_x = expand_heads(a_cum, 3)
    a_last_x = a_cum_x[CHUNK - 1:CHUNK, :]
    to_end_x = jnp.exp(a_last_x - a_cum_x)
    e_acum_x = jnp.exp(a_cum_x)
    chunk_decay_x = jnp.exp(a_last_x)

    lane = lax.broadcasted_iota(I32, (CHUNK, LANES), 1)
    lo_half = lane < SSD_HEADDIM
    hpg = SSD_HEADS // SSD_GROUPS
    gw = hpg * SSD_HEADDIM
    for g in range(SSD_GROUPS):
        bg = bm[:, g * SSD_STATE:(g + 1) * SSD_STATE].astype(BF16)
        cg = cm[:, g * SSD_STATE:(g + 1) * SSD_STATE].astype(BF16)
        cb = _dot_nt(cg, bg)
        prev = state_ref[:, g * gw:(g + 1) * gw]
        y_off = _dot(cg, prev.astype(BF16))
        y_parts = []
        xdec_parts = []
        for p in range(2):
            j = 2 * g + p
            xj = xs[:, j * LANES:(j + 1) * LANES]
            x_dt_f32 = xj * dt_x[:, j * LANES:(j + 1) * LANES]
            x_dt = x_dt_f32.astype(BF16)
            xdec_parts.append((x_dt_f32 * to_end_x[:, j * LANES:(j + 1) * LANES]).astype(BF16))
            yd = []
            for e in range(2):
                h = 2 * j + e
                diff = jnp.broadcast_to(a_cum[:, h:h + 1], (CHUNK, CHUNK)) - a_cum_t[h:h + 1, :]
                lmat = (cb * jnp.exp(jnp.where(causal, diff, -jnp.inf))).astype(BF16)
                yd.append(_dot(lmat, x_dt))
            y_diag = jnp.where(lo_half, yd[0], yd[1])
            y_parts.append(y_diag + y_off[:, p * LANES:(p + 1) * LANES] * e_acum_x[:, j * LANES:(j + 1) * LANES]
                           + dskip_ref[:, j * LANES:(j + 1) * LANES] * xj)
        xdec = jnp.concatenate(xdec_parts, axis=1)
        new_state = _dot_tn(bg, xdec)
        state_ref[:, g * gw:(g + 1) * gw] = chunk_decay_x[:, g * gw:(g + 1) * gw] * prev + new_state
        yg = jnp.concatenate(y_parts, axis=1) * _silu(z_ref[:, g * gw:(g + 1) * gw].astype(F32))
        ms = jnp.mean(yg * yg, axis=-1, keepdims=True)
        y_ref[:, g * gw:(g + 1) * gw] = (yg * lax.rsqrt(ms + NORM_EPS) * gn_ref[:, g * gw:(g + 1) * gw]).astype(BF16)


def _ssd_core(proj, dt_raw, conv_w, conv_b, dt_bias, a_log, d_skip, gnorm):
    L = proj.shape[0]
    nb = SSD_GROUPS * SSD_STATE
    pad = LANES - SSD_HEADS
    dtb = jnp.pad(dt_bias, (0, pad)).reshape(1, LANES)
    aneg = jnp.pad(-jnp.exp(a_log.astype(F32)), (0, pad)).reshape(1, LANES)
    dskip = jnp.repeat(d_skip.astype(F32), SSD_HEADDIM).reshape(1, SSD_DI)
    expand = (jnp.arange(LANES)[:, None] == jnp.arange(SSD_DI)[None, :] // SSD_HEADDIM).astype(BF16)
    full = lambda shape: pl.BlockSpec(shape, lambda c: (0, 0))
    return pl.pallas_call(
        _ssd_kernel,
        grid=(L // CHUNK,),
        in_specs=[
            pl.BlockSpec((CHUNK, SSD_DI), lambda c: (c, 0)),
            pl.BlockSpec((CHUNK, SSD_DI), lambda c: (c, 1)),
            pl.BlockSpec((CHUNK, nb), lambda c: (c, 2 * SSD_DI // nb)),
            pl.BlockSpec((CHUNK, nb), lambda c: (c, 2 * SSD_DI // nb + 1)),
            pl.BlockSpec((CHUNK, LANES), lambda c: (c, 0)),
            full((SSD_CONV, SSD_CONV_DIM)), full((1, SSD_CONV_DIM)), full((1, LANES)), full((1, LANES)),
            full((1, SSD_DI)), full((1, SSD_DI)), full((LANES, SSD_DI)),
        ],
        out_specs=pl.BlockSpec((CHUNK, SSD_DI), lambda c: (c, 0)),
        out_shape=jax.ShapeDtypeStruct((L, SSD_DI), BF16),
        scratch_shapes=[
            pltpu.VMEM((SSD_STATE, SSD_DI), F32),
            pltpu.VMEM((SUBLANES, SSD_CONV_DIM), F32),
        ],
        compiler_params=_cparams(("arbitrary",), VMEM_LIMIT_BYTES),
        name="ssd_core",
    )(proj, proj, proj, proj, dt_raw, conv_w, conv_b.reshape(1, SSD_CONV_DIM), dtb, aneg, dskip,
      gnorm.reshape(1, SSD_DI), expand)


def _ssd_layer(h, norm_g, w_in, conv_w, conv_b, dt_bias, a_log, d_skip, gnorm, w_out, final_g=None):
    proj, dt_raw = _norm_matmul(h, norm_g, w_in, SSD_DI + SSD_CONV_DIM, tn=1536)
    y = _ssd_core(proj, dt_raw, conv_w, conv_b, dt_bias, a_log, d_skip, gnorm)
    return _out_proj(y, w_out.astype(BF16), h, final_g)


def _ret_kernel(q_ref, k_ref, v_ref, g_ref, cos_ref, sin_ref, gn_ref, y_ref, state_ref):
    @pl.when(pl.program_id(0) == 0)
    def _():
        state_ref[...] = jnp.zeros_like(state_ref)

    cos = cos_ref[...]
    sin = sin_ref[...]
    half = RET_DK // 2
    rowf = lax.broadcasted_iota(I32, (CHUNK, LANES), 0).astype(F32)
    row = lax.broadcasted_iota(I32, (CHUNK, CHUNK), 0)
    col = lax.broadcasted_iota(I32, (CHUNK, CHUNK), 1)
    diff = (row - col).astype(F32)

    def rope(ref, h, scale):
        x1 = ref[:, h * RET_DK:h * RET_DK + half].astype(F32)
        x2 = ref[:, h * RET_DK + half:(h + 1) * RET_DK].astype(F32)
        return (x1 * cos - x2 * sin) * scale, (x2 * cos + x1 * sin) * scale

    for h in range(RET_HEADS):
        log_g = float(np.log1p(-np.exp2(np.float32(-5.0 - h))))
        q1, q2 = rope(q_ref, h, 1.0)
        k1, k2 = rope(k_ref, h, RET_DK ** -0.5)
        qr = jnp.concatenate([q1, q2], axis=1).astype(BF16)
        kr = jnp.concatenate([k1, k2], axis=1).astype(BF16)
        v = v_ref[:, h * RET_DV:(h + 1) * RET_DV].astype(BF16)
        intra = jnp.where(diff >= 0, jnp.exp(jnp.maximum(diff, 0.0) * log_g), 0.0)
        scores = (_dot_nt(qr, kr) * intra).astype(BF16)
        o = _dot(scores, v)
        q_decay = jnp.exp((rowf + 1.0) * log_g)
        q_dec = jnp.concatenate([q1 * q_decay, q2 * q_decay], axis=1).astype(BF16)
        prev = state_ref[h]
        o = o + _dot(q_dec, prev.astype(BF16))
        k_decay = jnp.exp((CHUNK - 1.0 - rowf) * log_g)
        k_dec = jnp.concatenate([k1 * k_decay, k2 * k_decay], axis=1).astype(BF16)
        state_ref[h] = math.exp(CHUNK * log_g) * prev + _dot_tn(k_dec, v)
        ms = jnp.mean(o * o, axis=-1, keepdims=True)
        on = o * lax.rsqrt(ms + NORM_EPS) * gn_ref[:, h * RET_DV:(h + 1) * RET_DV]
        y_ref[:, h * RET_DV:(h + 1) * RET_DV] = (on * _silu(g_ref[:, h * RET_DV:(h + 1) * RET_DV].astype(F32))).astype(BF16)


def _ret_core(proj, cos, sin, gnorm):
    L = proj.shape[0]
    qk = RET_HEADS * RET_DK
    return pl.pallas_call(
        _ret_kernel,
        grid=(L // CHUNK,),
        in_specs=[
            pl.BlockSpec((CHUNK, qk), lambda c: (c, 0)),
            pl.BlockSpec((CHUNK, qk), lambda c: (c, 1)),
            pl.BlockSpec((CHUNK, RET_DVTOT), lambda c: (c, 1)),
            pl.BlockSpec((CHUNK, RET_DVTOT), lambda c: (c, 2)),
            pl.BlockSpec((CHUNK, LANES), lambda c: (c, 0)),
            pl.BlockSpec((CHUNK, LANES), lambda c: (c, 0)),
            pl.BlockSpec((1, RET_DVTOT), lambda c: (0, 0)),
        ],
        out_specs=pl.BlockSpec((CHUNK, RET_DVTOT), lambda c: (c, 0)),
        out_shape=jax.ShapeDtypeStruct((L, RET_DVTOT), BF16),
        scratch_shapes=[pltpu.VMEM((RET_HEADS, RET_DK, RET_DV), F32)],
        compiler_params=_cparams(("arbitrary",), VMEM_LIMIT_BYTES),
        name="retention_core",
    )(proj, proj, proj, proj, cos, sin, gnorm.reshape(1, RET_DVTOT))


def _ret_layer(h, norm_g, w_in, gnorm, w_out, cos, sin):
    proj, _ = _norm_matmul(h, norm_g, w_in, RET_IN, tn=1536)
    y = _ret_core(proj, cos, sin, gnorm)
    return _out_proj(y, w_out.astype(BF16), h)


def _dsa_rope(x, c, s1, s2):
    half = DSA_HEADDIM // ROPE_FRACTION // 2
    parts = []
    for b in range(x.shape[1] // LANES):
        xb = x[:, b * LANES:(b + 1) * LANES]
        parts.append(xb * c + pltpu.roll(xb, LANES - half, 1) * s1 + pltpu.roll(xb, half, 1) * s2)
    return parts[0] if len(parts) == 1 else jnp.concatenate(parts, axis=1)


def _dsa_prep_kernel(q_ref, qi_ref, kv_ref, kw_ref, c_ref, s1_ref, s2_ref, kn_ref,
                     qx_ref, qix_ref, ko_ref, v1_ref, kid_ref, wo_ref):
    c = c_ref[...]
    s1 = s1_ref[...]
    s2 = s2_ref[...]
    rows = c.shape[0]
    lane = lax.broadcasted_iota(I32, (rows, LANES), 1)
    lo_half = lane < DSA_HEADDIM
    hi_half = jnp.logical_not(lo_half)

    def expand(x, scale, out_ref, target_half):
        for h in range(x.shape[1] // DSA_HEADDIM):
            blk = x[:, (h // 2) * LANES:(h // 2 + 1) * LANES] * scale
            if h % 2 != target_half(h):
                blk = pltpu.roll(blk, DSA_HEADDIM, 1)
            keep = lo_half if target_half(h) == 0 else hi_half
            out_ref[:, h * LANES:(h + 1) * LANES] = jnp.where(keep, blk, 0.0).astype(BF16)

    expand(_dsa_rope(q_ref[...].astype(F32), c, s1, s2), LOG2E * DSA_HEADDIM ** -0.5, qx_ref, lambda h: (h // DSA_GQA) % 2)
    expand(_dsa_rope(qi_ref[...].astype(F32), c, s1, s2), IDX_DIM ** -0.5, qix_ref, lambda h: h % 2)
    kwidth = DSA_KV_HEADS * DSA_HEADDIM
    ko_ref[...] = _dsa_rope(kv_ref[:, 0:kwidth].astype(F32), c, s1, s2).astype(BF16)
    for n in range(DSA_KV_HEADS):
        vb = kv_ref[:, kwidth + (n // 2) * LANES:kwidth + (n // 2 + 1) * LANES].astype(F32)
        if n % 2 == 1:
            vb = pltpu.roll(vb, DSA_HEADDIM, 1)
        v1_ref[n] = jnp.where(lo_half, vb, 1.0).astype(BF16)
    kw = kw_ref[...]
    ki = jnp.where(lo_half, kw, 0.0)
    ms = jnp.sum(ki * ki, axis=-1, keepdims=True) * (1.0 / IDX_DIM)
    kin = ki * lax.rsqrt(ms + NORM_EPS) * kn_ref[...]
    kir = jnp.where(lo_half, _dsa_rope(kin, c, s1, s2), 0.0)
    kid_ref[...] = (kir + pltpu.roll(kir, DSA_HEADDIM, 1)).astype(BF16)
    wo_ref[...] = jnp.where(lane < IDX_HEADS, pltpu.roll(kw, DSA_HEADDIM, 1) * IDX_HEADS ** -0.5, 0.0)


def _dsa_prep(proj, kw, c, s1, s2, idx_knorm, tl=512):
    L = proj.shape[0]
    kn = jnp.pad(idx_knorm.astype(F32), (0, LANES - IDX_DIM)).reshape(1, LANES)
    tok = lambda w, j: pl.BlockSpec((tl, w), lambda i: (i, j))
    return pl.pallas_call(
        _dsa_prep_kernel,
        grid=(L // tl,),
        in_specs=[
            tok(DSA_WIDTH, 0), tok(DSA_WIDTH, 1), tok(512, 6), tok(LANES, 0),
            tok(LANES, 0), tok(LANES, 0), tok(LANES, 0),
            pl.BlockSpec((1, LANES), lambda i: (0, 0)),
        ],
        out_specs=[
            tok(DSA_HEADS * LANES, 0), tok(IDX_HEADS * LANES, 0), tok(256, 0),
            pl.BlockSpec((DSA_KV_HEADS, tl, LANES), lambda i: (0, i, 0)),
            tok(LANES, 0), tok(LANES, 0),
        ],
        out_shape=[
            jax.ShapeDtypeStruct((L, DSA_HEADS * LANES), BF16),
            jax.ShapeDtypeStruct((L, IDX_HEADS * LANES), BF16),
            jax.ShapeDtypeStruct((L, 256), BF16),
            jax.ShapeDtypeStruct((DSA_KV_HEADS, L, LANES), BF16),
            jax.ShapeDtypeStruct((L, LANES), BF16),
            jax.ShapeDtypeStruct((L, LANES), F32),
        ],
        compiler_params=_cparams(("parallel",), VMEM_LIMIT_BYTES),
        name="dsa_prep",
    )(proj, proj, proj, kw, c, s1, s2, kn)


def _bf16_pattern(p16):
    return lax.bitcast_convert_type(lax.shift_left(p16, 16), F32)


def _dsa_main_kernel(qx_ref, qix_ref, w_ref, g_ref, kid_ref, k_ref, v1_ref, y_ref,
                     key_ref, sb_ref, wb_ref, m_ref, acc_ref, *, topk):
    i = pl.program_id(0)
    tq = DSA_QBLOCK
    ts, tc, ta = DSA_SCORE_TILE, DSA_COUNT_TILE, DSA_ATT_TILE
    t_cover = max(ts, tc, ta)
    end = i * tq + tq
    n_score = (end + ts - 1) // ts
    n_cover = (end + t_cover - 1) // t_cover * (t_cover // ts)
    n_count = (end + tc - 1) // tc
    n_att = (end + ta - 1) // ta
    lane = lax.broadcasted_iota(I32, (tq, LANES), 1)
    lo_half = lane < DSA_HEADDIM
    ftopk = float(topk)

    for h in range(IDX_HEADS):
        wb_ref[h] = jnp.broadcast_to(w_ref[:, h:h + 1], (tq, LANES))

    def stacked(ref, h0, nh):
        return jnp.concatenate([ref[:, h * LANES:(h + 1) * LANES] for h in range(h0, h0 + nh)], axis=0)

    q_pos = i * tq + lax.broadcasted_iota(I32, (tq, ts), 0)
    col_in_tile = lax.broadcasted_iota(I32, (tq, ts), 1)

    def score_tile(t, carry):
        k0 = pl.multiple_of(t * ts, ts)
        kt = kid_ref[pl.ds(k0, ts), :]
        acc = jnp.zeros((tq, ts), F32)
        hc = IDX_HCHUNK
        for c in range(IDX_HEADS // hc):
            s_c = _dot_nt(stacked(qix_ref, c * hc, hc), kt)
            for hh in range(hc):
                wt = jnp.concatenate([wb_ref[c * hc + hh]] * (ts // LANES), axis=1)
                acc = acc + wt * jnp.maximum(s_c[hh * tq:(hh + 1) * tq, :], 0.0)
        acc = jnp.where(jnp.abs(acc) < F32_TINY, 0.0, acc)
        bits = lax.bitcast_convert_type(acc, I32)
        key = jnp.where(bits < 0, bits ^ jnp.int32(0x7FFFFFFF), bits)
        valid = k0 + col_in_tile <= q_pos
        key_ref[:, pl.ds(k0, ts)] = jnp.where(valid, key, INT_MIN)
        trunc = lax.bitcast_convert_type(bits & jnp.int32(-65536), F32)
        sb_ref[:, pl.ds(k0, ts)] = jnp.where(valid, trunc, -jnp.inf).astype(BF16)
        return carry

    def masked_tile(t, carry):
        k0 = pl.multiple_of(t * ts, ts)
        key_ref[:, pl.ds(k0, ts)] = jnp.full((tq, ts), INT_MIN, I32)
        sb_ref[:, pl.ds(k0, ts)] = jnp.full((tq, ts), -jnp.inf, BF16)
        return carry

    lax.fori_loop(0, n_score, score_tile, 0)
    lax.fori_loop(n_score, n_cover, masked_tile, 0)

    ncb = tc // LANES

    def count_ge_bf16(cand_f32):
        cbt = jnp.concatenate([cand_f32.astype(BF16)] * ncb, axis=1)

        def body(t, hits):
            k0 = pl.multiple_of(t * tc, tc)
            hit = jnp.where(sb_ref[:, pl.ds(k0, tc)] >= cbt, jnp.ones((), BF16), jnp.zeros((), BF16))
            return hits + functools.reduce(jnp.add, [hit[:, b * LANES:(b + 1) * LANES] for b in range(ncb)])

        hits = lax.fori_loop(0, n_count, body, jnp.zeros((tq, LANES), BF16))
        return jnp.broadcast_to(jnp.sum(hits.astype(F32), axis=-1, keepdims=True), (tq, LANES))

    def hi_pattern(k16):
        return jnp.where(k16 >= 0, k16, k16 ^ jnp.int32(0x7FFF))

    def stage1(b, carry):
        t16, c_at = carry
        cand = t16 + lax.shift_left(jnp.int32(1), jnp.int32(15) - b)
        cnt = count_ge_bf16(_bf16_pattern(hi_pattern(cand)))
        ok = cnt >= ftopk
        return jnp.where(ok, cand, t16), jnp.where(ok, cnt, c_at)

    t16, c_at = lax.fori_loop(0, 16, stage1, (jnp.full((tq, LANES), -32768, I32), jnp.full((tq, LANES), 2.0 * L_MAX, F32)))
    few_valid = t16 <= -32768

    def unsettled(c):
        return jnp.max(jnp.where(few_valid | (c == ftopk), 0.0, 1.0)) > 0.0

    t16_t = jnp.concatenate([t16] * (ts // LANES), axis=1)

    def remap_tile(t, carry):
        k0 = pl.multiple_of(t * ts, ts)
        key = key_ref[:, pl.ds(k0, ts)]
        hi = lax.shift_right_arithmetic(key, 16)
        mid = (lax.shift_right_logical(key, 2) & jnp.int32(0x3FFF)) + jnp.int32(MID_BASE)
        pat = jnp.where(hi > t16_t, jnp.int32(MID_ABOVE), jnp.where(hi == t16_t, mid, 0))
        sb_ref[:, pl.ds(k0, ts)] = _bf16_pattern(pat).astype(BF16)
        return carry

    lax.fori_loop(0, n_cover, remap_tile, 0)

    def stage2(carry):
        b, u14, c = carry
        cand = u14 + lax.shift_left(jnp.int32(1), jnp.int32(13) - b)
        cnt = count_ge_bf16(_bf16_pattern(cand + jnp.int32(MID_BASE)))
        ok = cnt >= ftopk
        return b + 1, jnp.where(ok, cand, u14), jnp.where(ok, cnt, c)

    _, u14, c_at = lax.while_loop(lambda carry: jnp.logical_and(carry[0] < 14, unsettled(carry[2])), stage2,
                                  (jnp.int32(0), jnp.zeros((tq, LANES), I32), c_at))
    thr = lax.shift_left(t16, 16) + lax.shift_left(u14, 2)

    def count_ge_key(cand):
        def body(t, cnts):
            k0 = pl.multiple_of(t * tc, tc)
            cnts = list(cnts)
            for s in range(ncb):
                kk = key_ref[:, pl.ds(k0 + s * LANES, LANES)]
                cnts[s % 2] = cnts[s % 2] + jnp.where(kk >= cand, 1.0, 0.0)
            return tuple(cnts)

        z = jnp.zeros((tq, LANES), F32)
        c0, c1 = lax.fori_loop(0, n_count, body, (z, z))
        return jnp.broadcast_to(jnp.sum(c0 + c1, axis=-1, keepdims=True), (tq, LANES))

    def stage3(carry):
        k, thr, c = carry
        cand = thr + lax.shift_right_logical(jnp.int32(2), k)
        cnt = count_ge_key(cand)
        ok = cnt >= ftopk
        return k + 1, jnp.where(ok, cand, thr), jnp.where(ok, cnt, c)

    _, thr, c_at = lax.while_loop(lambda carry: jnp.logical_and(carry[0] < 2, unsettled(carry[2])), stage3,
                                  (jnp.int32(0), thr, c_at))
    thr = jnp.where(few_valid, INT_MIN + 1, jnp.maximum(thr, INT_MIN + 1))
    nab = ta // LANES
    thr_b = jnp.concatenate([thr] * nab, axis=1)

    rows = DSA_GQA * tq
    m_ref[...] = jnp.full(m_ref.shape, M_INIT, F32)
    acc_ref[...] = jnp.zeros(acc_ref.shape, F32)

    def attend(t, carry):
        k0 = pl.multiple_of(t * ta, ta)
        bias = jnp.where(key_ref[:, pl.ds(k0, ta)] >= thr_b, 0.0, NEG_BIG)
        bias4 = jnp.concatenate([bias] * DSA_GQA, axis=0)
        for n in range(DSA_KV_HEADS):
            kt = k_ref[pl.ds(k0, ta), (n // 2) * LANES:(n // 2 + 1) * LANES]
            s = _dot_nt(stacked(qx_ref, n * DSA_GQA, DSA_GQA), kt) + bias4
            blocks = [s[:, b * LANES:(b + 1) * LANES] for b in range(nab)]
            m_old = m_ref[n]
            m_new = jnp.maximum(m_old, jnp.max(functools.reduce(jnp.maximum, blocks), axis=-1, keepdims=True))
            p = jnp.concatenate([jnp.exp2(blk - m_new) for blk in blocks], axis=1).astype(BF16)
            acc_ref[n] = jnp.exp2(m_old - m_new) * acc_ref[n] + _dot(p, v1_ref[n, pl.ds(k0, ta), :])
            m_ref[n] = m_new
        return carry

    lax.fori_loop(0, n_att, attend, 0)
    outs = []
    for n in range(DSA_KV_HEADS):
        acc = acc_ref[n]
        o = acc * (1.0 / pltpu.roll(acc, DSA_HEADDIM, 1))
        outs.extend(o[e * tq:(e + 1) * tq, :] for e in range(DSA_GQA))
    for j in range(DSA_HEADS // 2):
        blk = jnp.where(lo_half, outs[2 * j], pltpu.roll(outs[2 * j + 1], DSA_HEADDIM, 1))
        cols = slice(j * LANES, (j + 1) * LANES)
        y_ref[:, cols] = (blk * _silu(g_ref[:, cols].astype(F32))).astype(BF16)


def _dsa_main(proj, qx, qix, kr, v1, kidup, wi, topk):
    L = proj.shape[0]
    tq = DSA_QBLOCK
    resident = lambda shape: pl.BlockSpec(shape, lambda i: (0,) * len(shape), pipeline_mode=pl.Buffered(1))
    return pl.pallas_call(
        functools.partial(_dsa_main_kernel, topk=topk),
        grid=(L // tq,),
        in_specs=[
            pl.BlockSpec((tq, DSA_HEADS * LANES), lambda i: (i, 0)),
            pl.BlockSpec((tq, IDX_HEADS * LANES), lambda i: (i, 0)),
            pl.BlockSpec((tq, LANES), lambda i: (i, 0)),
            pl.BlockSpec((tq, DSA_WIDTH), lambda i: (i, 2)),
            resident((L, LANES)),
            resident((L, 256)),
            resident((DSA_KV_HEADS, L, LANES)),
        ],
        out_specs=pl.BlockSpec((tq, DSA_WIDTH), lambda i: (i, 0)),
        out_shape=jax.ShapeDtypeStruct((L, DSA_WIDTH), BF16),
        scratch_shapes=[
            pltpu.VMEM((tq, L), I32),
            pltpu.VMEM((tq, L), BF16),
            pltpu.VMEM((IDX_HEADS, tq, LANES), F32),
            pltpu.VMEM((DSA_KV_HEADS, DSA_GQA * tq, LANES), F32),
            pltpu.VMEM((DSA_KV_HEADS, DSA_GQA * tq, LANES), F32),
        ],
        compiler_params=_cparams(("arbitrary",), VMEM_LIMIT_BYTES),
        name="dsa_index_attend",
    )(qx, qix, wi, proj, kidup, kr, v1)


def _dsa_layer(h, norm_g, w_in, idx_knorm, w_out, c, s1, s2):
    L = h.shape[0]
    q, k, v, g, qi, ki, wi = jnp.split(w_in, np.cumsum([1024, 256, 256, 1024, 1024, 64, 16])[:-1].tolist(), axis=1)
    w = jnp.concatenate([q, qi, g, k, v, ki, wi], axis=1)
    proj, kw = _norm_matmul(h, norm_g, w, DSA_IN_MAIN, tn=DSA_IN_MAIN // 2)
    qx, qix, kr, v1, kidup, wsc = _dsa_prep(proj, kw, c, s1, s2, idx_knorm)
    y = _dsa_main(proj, qx, qix, kr, v1, kidup, wsc, topk=min(TOPK_MAX, L // 4))
    return _out_proj(y, w_out.astype(BF16), h)


def kernel(x, positions, l0_norm, l0_w_in, l0_conv_w, l0_conv_b, l0_dt_bias, l0_a_log, l0_d_skip, l0_gnorm, l0_w_out, l1_norm, l1_w_in, l1_gnorm, l1_w_out, l2_norm, l2_w_in, l2_idx_knorm, l2_w_out, l3_norm, l3_w_in, l3_conv_w, l3_conv_b, l3_dt_bias, l3_a_log, l3_d_skip, l3_gnorm, l3_w_out, final_norm):
    b, L, d = x.shape
    outs = []
    for bi in range(b):
        h = x[bi]
        rc, rs, dc, ds1, ds2 = _rope_tables(positions[bi].astype(F32))
        h = _ssd_layer(h, l0_norm, l0_w_in, l0_conv_w, l0_conv_b, l0_dt_bias, l0_a_log, l0_d_skip, l0_gnorm, l0_w_out)
        h = _ret_layer(h, l1_norm, l1_w_in, l1_gnorm, l1_w_out, rc, rs)
        h = _dsa_layer(h, l2_norm, l2_w_in, l2_idx_knorm, l2_w_out, dc, ds1, ds2)
        h = _ssd_layer(h, l3_norm, l3_w_in, l3_conv_w, l3_conv_b, l3_dt_bias, l3_a_log, l3_d_skip, l3_gnorm, l3_w_out,
                       final_g=final_norm)
        outs.append(h)
    return jnp.stack(outs, axis=0)
```

```python
import functools
import math

import numpy as np
import jax
import jax.numpy as jnp
from jax import lax
from jax.experimental import pallas as pl
from jax.experimental.pallas import tpu as pltpu

F32 = jnp.float32
BF16 = jnp.bfloat16
I32 = jnp.int32

LANES = 128
SUBLANES = 8
VMEM_LIMIT_BYTES = 56 * 1024 * 1024

D_MODEL = 1024
NORM_EPS = 1e-6
CHUNK = 128

SSD_DI = 2048
SSD_HEADDIM = 64
SSD_HEADS = 32
SSD_GROUPS = 8
SSD_STATE = 128
SSD_CONV = 4
SSD_CONV_DIM = 4096

RET_HEADS = 4
RET_DK = 256
RET_DV = 512
RET_DVTOT = 2048
RET_THETA = 10000.0
RET_IN = 2 * RET_HEADS * RET_DK + 2 * RET_DVTOT

DSA_HEADS = 16
DSA_HEADDIM = 64
DSA_KV_HEADS = 4
DSA_GQA = 4
DSA_WIDTH = 1024
IDX_HEADS = 16
IDX_DIM = 64
TOPK_MAX = 256
ROPE_THETA = 500000.0
ROPE_FRACTION = 4
DSA_IN_MAIN = 3584
IN_PROJ_TM = 2048
DSA_QBLOCK = 128
DSA_SCORE_TILE = 1024
DSA_COUNT_TILE = 1024
DSA_ATT_TILE = 1024
IDX_HCHUNK = 16
LOG2E = 1.4426950408889634
F32_TINY = 1.1754943508222875e-38
MID_BASE = 0x0080
MID_ABOVE = 0x7F00

L_MAX = 32768.0
INT_MIN = -(2 ** 31)
NEG_BIG = -1e30
M_INIT = -1e29


def _cparams(sem, vmem=None):
    return pltpu.CompilerParams(dimension_semantics=sem, vmem_limit_bytes=vmem)


def _sigmoid(x):
    return 1.0 / (1.0 + jnp.exp(-x))


def _silu(x):
    return x * _sigmoid(x)


def _dot(a, b):
    return jnp.dot(a, b, preferred_element_type=F32)


def _dot_nt(a, b):
    return lax.dot_general(a, b, (((1,), (1,)), ((), ())), preferred_element_type=F32)


def _dot_tn(a, b):
    return lax.dot_general(a, b, (((0,), (0,)), ((), ())), preferred_element_type=F32)


def _norm_matmul_kernel(h_ref, g_ref, w_ref, *rest, has_tail):
    if has_tail:
        wt_ref, o_ref, t_ref, hn_ref = rest
    else:
        o_ref, hn_ref = rest

    @pl.when(pl.program_id(1) == 0)
    def _():
        x = h_ref[...]
        ms = jnp.mean(x * x, axis=-1, keepdims=True)
        hn_ref[...] = (x * lax.rsqrt(ms + NORM_EPS) * g_ref[...]).astype(BF16)
        if has_tail:
            t_ref[...] = _dot(hn_ref[...], wt_ref[...])

    o_ref[...] = _dot(hn_ref[...], w_ref[...]).astype(BF16)


def _norm_matmul(h, g, w, n_main, tn, tm=IN_PROJ_TM):
    L, d = h.shape
    w_main = w[:, :n_main].astype(BF16)
    n_tail = w.shape[1] - n_main
    in_specs = [
        pl.BlockSpec((tm, d), lambda i, j: (i, 0)),
        pl.BlockSpec((1, d), lambda i, j: (0, 0)),
        pl.BlockSpec((d, tn), lambda i, j: (0, j)),
    ]
    out_specs = [pl.BlockSpec((tm, tn), lambda i, j: (i, j))]
    out_shape = [jax.ShapeDtypeStruct((L, n_main), BF16)]
    args = [h, g.reshape(1, d), w_main]
    if n_tail:
        in_specs.append(pl.BlockSpec((d, LANES), lambda i, j: (0, 0)))
        out_specs.append(pl.BlockSpec((tm, LANES), lambda i, j: (i, 0)))
        out_shape.append(jax.ShapeDtypeStruct((L, LANES), F32))
        args.append(jnp.pad(w[:, n_main:], ((0, 0), (0, LANES - n_tail))).astype(BF16))
    outs = pl.pallas_call(
        functools.partial(_norm_matmul_kernel, has_tail=bool(n_tail)),
        grid=(L // tm, n_main // tn),
        in_specs=in_specs,
        out_specs=out_specs,
        out_shape=out_shape,
        scratch_shapes=[pltpu.VMEM((tm, d), BF16)],
        compiler_params=_cparams(("parallel", "arbitrary"), VMEM_LIMIT_BYTES),
        name="norm_in_proj",
    )(*args)
    return outs if n_tail else (outs[0], None)


def _out_proj_kernel(y_ref, w_ref, h_ref, o_ref):
    o_ref[...] = h_ref[...] + _dot(y_ref[...], w_ref[...])


def _out_proj_final_kernel(y_ref, w_ref, h_ref, g_ref, o_ref):
    x = h_ref[...] + _dot(y_ref[...], w_ref[...])
    ms = jnp.mean(x * x, axis=-1, keepdims=True)
    o_ref[...] = x * lax.rsqrt(ms + NORM_EPS) * g_ref[...]


def _out_proj(y_bf16, w_bf16, h, final_g=None, tm=1024):
    L, k = y_bf16.shape
    d = w_bf16.shape[1]
    in_specs = [
        pl.BlockSpec((tm, k), lambda i: (i, 0)),
        pl.BlockSpec((k, d), lambda i: (0, 0)),
        pl.BlockSpec((tm, d), lambda i: (i, 0)),
    ]
    args = [y_bf16, w_bf16, h]
    body = _out_proj_kernel
    if final_g is not None:
        in_specs.append(pl.BlockSpec((1, d), lambda i: (0, 0)))
        args.append(final_g.reshape(1, d))
        body = _out_proj_final_kernel
    return pl.pallas_call(
        body,
        grid=(L // tm,),
        in_specs=in_specs,
        out_specs=pl.BlockSpec((tm, d), lambda i: (i, 0)),
        out_shape=jax.ShapeDtypeStruct((L, d), F32),
        compiler_params=_cparams(("parallel",), VMEM_LIMIT_BYTES),
        name="out_proj_residual",
    )(*args)


def _rope_table_kernel(pos_ref, inv_ret_ref, inv_dsa_ref, rc_ref, rs_ref, dc_ref, ds1_ref, ds2_ref):
    pos = pos_ref[...]
    ang = pos * inv_ret_ref[...]
    rc_ref[...] = jnp.cos(ang)
    rs_ref[...] = jnp.sin(ang)
    ang_d = pos * inv_dsa_ref[...]
    lane = lax.broadcasted_iota(I32, ang_d.shape, 1) % DSA_HEADDIM
    half = DSA_HEADDIM // ROPE_FRACTION // 2
    c = jnp.cos(ang_d)
    s = jnp.sin(ang_d)
    dc_ref[...] = jnp.where(lane < 2 * half, c, 1.0)
    ds1_ref[...] = jnp.where(lane < half, -s, 0.0)
    ds2_ref[...] = jnp.where((lane >= half) & (lane < 2 * half), s, 0.0)


def _rope_tables(positions_f32, tl=1024):
    L = positions_f32.shape[0]
    inv_ret = 1.0 / (jnp.float32(RET_THETA) ** jnp.linspace(0.0, 1.0, RET_DK // 2, dtype=F32))
    r = DSA_HEADDIM // ROPE_FRACTION
    inv_dsa = jnp.float32(ROPE_THETA) ** (-jnp.arange(0, r, 2, dtype=F32) / r)
    inv_dsa_lanes = jnp.tile(inv_dsa, LANES // inv_dsa.shape[0])
    tab = jax.ShapeDtypeStruct((L, LANES), F32)
    spec = pl.BlockSpec((tl, LANES), lambda i: (i, 0))
    cst = pl.BlockSpec((1, LANES), lambda i: (0, 0))
    return pl.pallas_call(
        _rope_table_kernel,
        grid=(L // tl,),
        in_specs=[pl.BlockSpec((tl, 1), lambda i: (i, 0)), cst, cst],
        out_specs=[spec] * 5,
        out_shape=[tab] * 5,
        compiler_params=_cparams(("parallel",)),
        name="rope_tables",
    )(positions_f32.reshape(L, 1), inv_ret.reshape(1, LANES), inv_dsa_lanes.reshape(1, LANES))


def _ssd_kernel(z_ref, x_ref, b_ref, c_ref, dt_ref, cw_ref, cb_ref, dtb_ref, aneg_ref, dskip_ref, gn_ref, ex_ref,
                y_ref, state_ref, tail_ref):
    @pl.when(pl.program_id(0) == 0)
    def _():
        state_ref[...] = jnp.zeros_like(state_ref)
        tail_ref[...] = jnp.zeros_like(tail_ref)

    row8 = lax.broadcasted_iota(I32, (SUBLANES, LANES), 0)

    def conv_silu(cur_ref, col0, width):
        cur = cur_ref[...].astype(F32)
        prev = tail_ref[:, col0:col0 + width]
        acc = cb_ref[:, col0:col0 + width] + cw_ref[SSD_CONV - 1:SSD_CONV, col0:col0 + width] * cur
        for j in range(SSD_CONV - 1):
            shift = SSD_CONV - 1 - j
            rolled = pltpu.roll(cur, shift, 0)
            head = jnp.where(jnp.concatenate([row8] * (width // LANES), axis=1) < shift,
                             pltpu.roll(prev, shift, 0), rolled[0:SUBLANES, :])
            shifted = jnp.concatenate([head, rolled[SUBLANES:, :]], axis=0)
            acc = acc + cw_ref[j:j + 1, col0:col0 + width] * shifted
        tail_ref[:, col0:col0 + width] = cur[CHUNK - SUBLANES:CHUNK, :]
        return _silu(acc)

    xs = conv_silu(x_ref, 0, SSD_DI)
    bm = conv_silu(b_ref, SSD_DI, SSD_GROUPS * SSD_STATE)
    cm = conv_silu(c_ref, SSD_DI + SSD_GROUPS * SSD_STATE, SSD_GROUPS * SSD_STATE)

    raw = dt_ref[...] + dtb_ref[...]
    dt = jnp.maximum(raw, 0.0) + jnp.log1p(jnp.exp(-jnp.abs(raw)))
    a = dt * aneg_ref[...]
    row = lax.broadcasted_iota(I32, (CHUNK, CHUNK), 0)
    col = lax.broadcasted_iota(I32, (CHUNK, CHUNK), 1)
    causal = row >= col
    a_cum = jnp.dot(causal.astype(F32), a, precision=lax.Precision.HIGHEST, preferred_element_type=F32)
    a_cum_t = a_cum.T

    def expand_heads(m, terms):
        out = None
        for _ in range(terms):
            part = m.astype(BF16)
            m = m - part.astype(F32)
            d = _dot(part, ex_ref[...])
            out = d if out is None else out + d
        return out

    dt_x = expand_heads(dt, 2)
    a_cum_x = expand_heads(a_cum, 3)
    a_last_x = a_cum_x[CHUNK - 1:CHUNK, :]
    to_end_x = jnp.exp(a_last_x - a_cum_x)
    e_acum_x = jnp.exp(a_cum_x)
    chunk_decay_x = jnp.exp(a_last_x)

    lane = lax.broadcasted_iota(I32, (CHUNK, LANES), 1)
    lo_half = lane < SSD_HEADDIM
    hpg = SSD_HEADS // SSD_GROUPS
    gw = hpg * SSD_HEADDIM
    for g in range(SSD_GROUPS):
        bg = bm[:, g * SSD_STATE:(g + 1) * SSD_STATE].astype(BF16)
        cg = cm[:, g * SSD_STATE:(g + 1) * SSD_STATE].astype(BF16)
        cb = _dot_nt(cg, bg)
        prev = state_ref[:, g * gw:(g + 1) * gw]
        y_off = _dot(cg, prev.astype(BF16))
        y_parts = []
        xdec_parts = []
        for p in range(2):
            j = 2 * g + p
            xj = xs[:, j * LANES:(j + 1) * LANES]
            x_dt_f32 = xj * dt_x[:, j * LANES:(j + 1) * LANES]
            x_dt = x_dt_f32.astype(BF16)
            xdec_parts.append((x_dt_f32 * to_end_x[:, j * LANES:(j + 1) * LANES]).astype(BF16))
            yd = []
            for e in range(2):
                h = 2 * j + e
                diff = jnp.broadcast_to(a_cum[:, h:h + 1], (CHUNK, CHUNK)) - a_cum_t[h:h + 1, :]
                lmat = (cb * jnp.exp(jnp.where(causal, diff, -jnp.inf))).astype(BF16)
                yd.append(_dot(lmat, x_dt))
            y_diag = jnp.where(lo_half, yd[0], yd[1])
            y_parts.append(y_diag + y_off[:, p * LANES:(p + 1) * LANES] * e_acum_x[:, j * LANES:(j + 1) * LANES]
                           + dskip_ref[:, j * LANES:(j + 1) * LANES] * xj)
        xdec = jnp.concatenate(xdec_parts, axis=1)
        new_state = _dot_tn(bg, xdec)
        state_ref[:, g * gw:(g + 1) * gw] = chunk_decay_x[:, g * gw:(g + 1) * gw] * prev + new_state
        yg = jnp.concatenate(y_parts, axis=1) * _silu(z_ref[:, g * gw:(g + 1) * gw].astype(F32))
        ms = jnp.mean(yg * yg, axis=-1, keepdims=True)
        y_ref[:, g * gw:(g + 1) * gw] = (yg * lax.rsqrt(ms + NORM_EPS) * gn_ref[:, g * gw:(g + 1) * gw]).astype(BF16)


def _ssd_core(proj, dt_raw, conv_w, conv_b, dt_bias, a_log, d_skip, gnorm):
    L = proj.shape[0]
    nb = SSD_GROUPS * SSD_STATE
    pad = LANES - SSD_HEADS
    dtb = jnp.pad(dt_bias, (0, pad)).reshape(1, LANES)
    aneg = jnp.pad(-jnp.exp(a_log.astype(F32)), (0, pad)).reshape(1, LANES)
    dskip = jnp.repeat(d_skip.astype(F32), SSD_HEADDIM).reshape(1, SSD_DI)
    expand = (jnp.arange(LANES)[:, None] == jnp.arange(SSD_DI)[None, :] // SSD_HEADDIM).astype(BF16)
    full = lambda shape: pl.BlockSpec(shape, lambda c: (0, 0))
    return pl.pallas_call(
        _ssd_kernel,
        grid=(L // CHUNK,),
        in_specs=[
            pl.BlockSpec((CHUNK, SSD_DI), lambda c: (c, 0)),
            pl.BlockSpec((CHUNK, SSD_DI), lambda c: (c, 1)),
            pl.BlockSpec((CHUNK, nb), lambda c: (c, 2 * SSD_DI // nb)),
            pl.BlockSpec((CHUNK, nb), lambda c: (c, 2 * SSD_DI // nb + 1)),
            pl.BlockSpec((CHUNK, LANES), lambda c: (c, 0)),
            full((SSD_CONV, SSD_CONV_DIM)), full((1, SSD_CONV_DIM)), full((1, LANES)), full((1, LANES)),
            full((1, SSD_DI)), full((1, SSD_DI)), full((LANES, SSD_DI)),
        ],
        out_specs=pl.BlockSpec((CHUNK, SSD_DI), lambda c: (c, 0)),
        out_shape=jax.ShapeDtypeStruct((L, SSD_DI), BF16),
        scratch_shapes=[
            pltpu.VMEM((SSD_STATE, SSD_DI), F32),
            pltpu.VMEM((SUBLANES, SSD_CONV_DIM), F32),
        ],
        compiler_params=_cparams(("arbitrary",), VMEM_LIMIT_BYTES),
        name="ssd_core",
    )(proj, proj, proj, proj, dt_raw, conv_w, conv_b.reshape(1, SSD_CONV_DIM), dtb, aneg, dskip,
      gnorm.reshape(1, SSD_DI), expand)


def _ssd_layer(h, norm_g, w_in, conv_w, conv_b, dt_bias, a_log, d_skip, gnorm, w_out, final_g=None):
    proj, dt_raw = _norm_matmul(h, norm_g, w_in, SSD_DI + SSD_CONV_DIM, tn=1536)
    y = _ssd_core(proj, dt_raw, conv_w, conv_b, dt_bias, a_log, d_skip, gnorm)
    return _out_proj(y, w_out.astype(BF16), h, final_g)


def _ret_kernel(q_ref, k_ref, v_ref, g_ref, cos_ref, sin_ref, gn_ref, y_ref, state_ref):
    @pl.when(pl.program_id(0) == 0)
    def _():
        state_ref[...] = jnp.zeros_like(state_ref)

    cos = cos_ref[...]
    sin = sin_ref[...]
    half = RET_DK // 2
    rowf = lax.broadcasted_iota(I32, (CHUNK, LANES), 0).astype(F32)
    row = lax.broadcasted_iota(I32, (CHUNK, CHUNK), 0)
    col = lax.broadcasted_iota(I32, (CHUNK, CHUNK), 1)
    diff = (row - col).astype(F32)

    def rope(ref, h, scale):
        x1 = ref[:, h * RET_DK:h * RET_DK + half].astype(F32)
        x2 = ref[:, h * RET_DK + half:(h + 1) * RET_DK].astype(F32)
        return (x1 * cos - x2 * sin) * scale, (x2 * cos + x1 * sin) * scale

    for h in range(RET_HEADS):
        log_g = float(np.log1p(-np.exp2(np.float32(-5.0 - h))))
        q1, q2 = rope(q_ref, h, 1.0)
        k1, k2 = rope(k_ref, h, RET_DK ** -0.5)
        qr = jnp.concatenate([q1, q2], axis=1).astype(BF16)
        kr = jnp.concatenate([k1, k2], axis=1).astype(BF16)
        v = v_ref[:, h * RET_DV:(h + 1) * RET_DV].astype(BF16)
        intra = jnp.where(diff >= 0, jnp.exp(jnp.maximum(diff, 0.0) * log_g), 0.0)
        scores = (_dot_nt(qr, kr) * intra).astype(BF16)
        o = _dot(scores, v)
        q_decay = jnp.exp((rowf + 1.0) * log_g)
        q_dec = jnp.concatenate([q1 * q_decay, q2 * q_decay], axis=1).astype(BF16)
        prev = state_ref[h]
        o = o + _dot(q_dec, prev.astype(BF16))
        k_decay = jnp.exp((CHUNK - 1.0 - rowf) * log_g)
        k_dec = jnp.concatenate([k1 * k_decay, k2 * k_decay], axis=1).astype(BF16)
        state_ref[h] = math.exp(CHUNK * log_g) * prev + _dot_tn(k_dec, v)
        ms = jnp.mean(o * o, axis=-1, keepdims=True)
        on = o * lax.rsqrt(ms + NORM_EPS) * gn_ref[:, h * RET_DV:(h + 1) * RET_DV]
        y_ref[:, h * RET_DV:(h + 1) * RET_DV] = (on * _silu(g_ref[:, h * RET_DV:(h + 1) * RET_DV].astype(F32))).astype(BF16)


def _ret_core(proj, cos, sin, gnorm):
    L = proj.shape[0]
    qk = RET_HEADS * RET_DK
    return pl.pallas_call(
        _ret_kernel,
        grid=(L // CHUNK,),
        in_specs=[
            pl.BlockSpec((CHUNK, qk), lambda c: (c, 0)),
            pl.BlockSpec((CHUNK, qk), lambda c: (c, 1)),
            pl.BlockSpec((CHUNK, RET_DVTOT), lambda c: (c, 1)),
            pl.BlockSpec((CHUNK, RET_DVTOT), lambda c: (c, 2)),
            pl.BlockSpec((CHUNK, LANES), lambda c: (c, 0)),
            pl.BlockSpec((CHUNK, LANES), lambda c: (c, 0)),
            pl.BlockSpec((1, RET_DVTOT), lambda c: (0, 0)),
        ],
        out_specs=pl.BlockSpec((CHUNK, RET_DVTOT), lambda c: (c, 0)),
        out_shape=jax.ShapeDtypeStruct((L, RET_DVTOT), BF16),
        scratch_shapes=[pltpu.VMEM((RET_HEADS, RET_DK, RET_DV), F32)],
        compiler_params=_cparams(("arbitrary",), VMEM_LIMIT_BYTES),
        name="retention_core",
    )(proj, proj, proj, proj, cos, sin, gnorm.reshape(1, RET_DVTOT))


def _ret_layer(h, norm_g, w_in, gnorm, w_out, cos, sin):
    proj, _ = _norm_matmul(h, norm_g, w_in, RET_IN, tn=1536)
    y = _ret_core(proj, cos, sin, gnorm)
    return _out_proj(y, w_out.astype(BF16), h)


def _dsa_rope(x, c, s1, s2):
    half = DSA_HEADDIM // ROPE_FRACTION // 2
    parts = []
    for b in range(x.shape[1] // LANES):
        xb = x[:, b * LANES:(b + 1) * LANES]
        parts.append(xb * c + pltpu.roll(xb, LANES - half, 1) * s1 + pltpu.roll(xb, half, 1) * s2)
    return parts[0] if len(parts) == 1 else jnp.concatenate(parts, axis=1)


def _dsa_prep_kernel(q_ref, qi_ref, kv_ref, kw_ref, c_ref, s1_ref, s2_ref, kn_ref,
                     qx_ref, qix_ref, ko_ref, v1_ref, kid_ref, wo_ref):
    c = c_ref[...]
    s1 = s1_ref[...]
    s2 = s2_ref[...]
    rows = c.shape[0]
    lane = lax.broadcasted_iota(I32, (rows, LANES), 1)
    lo_half = lane < DSA_HEADDIM
    hi_half = jnp.logical_not(lo_half)

    def expand(x, scale, out_ref, target_half):
        for h in range(x.shape[1] // DSA_HEADDIM):
            blk = x[:, (h // 2) * LANES:(h // 2 + 1) * LANES] * scale
            if h % 2 != target_half(h):
                blk = pltpu.roll(blk, DSA_HEADDIM, 1)
            keep = lo_half if target_half(h) == 0 else hi_half
            out_ref[:, h * LANES:(h + 1) * LANES] = jnp.where(keep, blk, 0.0).astype(BF16)

    expand(_dsa_rope(q_ref[...].astype(F32), c, s1, s2), LOG2E * DSA_HEADDIM ** -0.5, qx_ref, lambda h: (h // DSA_GQA) % 2)
    expand(_dsa_rope(qi_ref[...].astype(F32), c, s1, s2), IDX_DIM ** -0.5, qix_ref, lambda h: h % 2)
    kwidth = DSA_KV_HEADS * DSA_HEADDIM
    ko_ref[...] = _dsa_rope(kv_ref[:, 0:kwidth].astype(F32), c, s1, s2).astype(BF16)
    for n in range(DSA_KV_HEADS):
        vb = kv_ref[:, kwidth + (n // 2) * LANES:kwidth + (n // 2 + 1) * LANES].astype(F32)
        if n % 2 == 1:
            vb = pltpu.roll(vb, DSA_HEADDIM, 1)
        v1_ref[n] = jnp.where(lo_half, vb, 1.0).astype(BF16)
    kw = kw_ref[...]
    ki = jnp.where(lo_half, kw, 0.0)
    ms = jnp.sum(ki * ki, axis=-1, keepdims=True) * (1.0 / IDX_DIM)
    kin = ki * lax.rsqrt(ms + NORM_EPS) * kn_ref[...]
    kir = jnp.where(lo_half, _dsa_rope(kin, c, s1, s2), 0.0)
    kid_ref[...] = (kir + pltpu.roll(kir, DSA_HEADDIM, 1)).astype(BF16)
    wo_ref[...] = jnp.where(lane < IDX_HEADS, pltpu.roll(kw, DSA_HEADDIM, 1) * IDX_HEADS ** -0.5, 0.0)


def _dsa_prep(proj, kw, c, s1, s2, idx_knorm, tl=512):
    L = proj.shape[0]
    kn = jnp.pad(idx_knorm.astype(F32), (0, LANES - IDX_DIM)).reshape(1, LANES)
    tok = lambda w, j: pl.BlockSpec((tl, w), lambda i: (i, j))
    return pl.pallas_call(
        _dsa_prep_kernel,
        grid=(L // tl,),
        in_specs=[
            tok(DSA_WIDTH, 0), tok(DSA_WIDTH, 1), tok(512, 6), tok(LANES, 0),
            tok(LANES, 0), tok(LANES, 0), tok(LANES, 0),
            pl.BlockSpec((1, LANES), lambda i: (0, 0)),
        ],
        out_specs=[
            tok(DSA_HEADS * LANES, 0), tok(IDX_HEADS * LANES, 0), tok(256, 0),
            pl.BlockSpec((DSA_KV_HEADS, tl, LANES), lambda i: (0, i, 0)),
            tok(LANES, 0), tok(LANES, 0),
        ],
        out_shape=[
            jax.ShapeDtypeStruct((L, DSA_HEADS * LANES), BF16),
            jax.ShapeDtypeStruct((L, IDX_HEADS * LANES), BF16),
            jax.ShapeDtypeStruct((L, 256), BF16),
            jax.ShapeDtypeStruct((DSA_KV_HEADS, L, LANES), BF16),
            jax.ShapeDtypeStruct((L, LANES), BF16),
            jax.ShapeDtypeStruct((L, LANES), F32),
        ],
        compiler_params=_cparams(("parallel",), VMEM_LIMIT_BYTES),
        name="dsa_prep",
    )(proj, proj, proj, kw, c, s1, s2, kn)


def _bf16_pattern(p16):
    return lax.bitcast_convert_type(lax.shift_left(p16, 16), F32)


def _dsa_main_kernel(qx_ref, qix_ref, w_ref, g_ref, kid_ref, k_ref, v1_ref, y_ref,
                     key_ref, sb_ref, wb_ref, m_ref, acc_ref, *, topk):
    i = pl.program_id(0)
    tq = DSA_QBLOCK
    ts, tc, ta = DSA_SCORE_TILE, DSA_COUNT_TILE, DSA_ATT_TILE
    t_cover = max(ts, tc, ta)
    end = i * tq + tq
    n_score = (end + ts - 1) // ts
    n_cover = (end + t_cover - 1) // t_cover * (t_cover // ts)
    n_count = (end + tc - 1) // tc
    n_att = (end + ta - 1) // ta
    lane = lax.broadcasted_iota(I32, (tq, LANES), 1)
    lo_half = lane < DSA_HEADDIM
    ftopk = float(topk)

    for h in range(IDX_HEADS):
        wb_ref[h] = jnp.broadcast_to(w_ref[:, h:h + 1], (tq, LANES))

    def stacked(ref, h0, nh):
        return jnp.concatenate([ref[:, h * LANES:(h + 1) * LANES] for h in range(h0, h0 + nh)], axis=0)

    q_pos = i * tq + lax.broadcasted_iota(I32, (tq, ts), 0)
    col_in_tile = lax.broadcasted_iota(I32, (tq, ts), 1)

    def score_tile(t, carry):
        k0 = pl.multiple_of(t * ts, ts)
        kt = kid_ref[pl.ds(k0, ts), :]
        acc = jnp.zeros((tq, ts), F32)
        hc = IDX_HCHUNK
        for c in range(IDX_HEADS // hc):
            s_c = _dot_nt(stacked(qix_ref, c * hc, hc), kt)
            for hh in range(hc):
                wt = jnp.concatenate([wb_ref[c * hc + hh]] * (ts // LANES), axis=1)
                acc = acc + wt * jnp.maximum(s_c[hh * tq:(hh + 1) * tq, :], 0.0)
        acc = jnp.where(jnp.abs(acc) < F32_TINY, 0.0, acc)
        bits = lax.bitcast_convert_type(acc, I32)
        key = jnp.where(bits < 0, bits ^ jnp.int32(0x7FFFFFFF), bits)
        valid = k0 + col_in_tile <= q_pos
        key_ref[:, pl.ds(k0, ts)] = jnp.where(valid, key, INT_MIN)
        trunc = lax.bitcast_convert_type(bits & jnp.int32(-65536), F32)
        sb_ref[:, pl.ds(k0, ts)] = jnp.where(valid, trunc, -jnp.inf).astype(BF16)
        return carry

    def masked_tile(t, carry):
        k0 = pl.multiple_of(t * ts, ts)
        key_ref[:, pl.ds(k0, ts)] = jnp.full((tq, ts), INT_MIN, I32)
        sb_ref[:, pl.ds(k0, ts)] = jnp.full((tq, ts), -jnp.inf, BF16)
        return carry

    lax.fori_loop(0, n_score, score_tile, 0)
    lax.fori_loop(n_score, n_cover, masked_tile, 0)

    ncb = tc // LANES

    def count_ge_bf16(cand_f32):
        cbt = jnp.concatenate([cand_f32.astype(BF16)] * ncb, axis=1)

        def body(t, hits):
            k0 = pl.multiple_of(t * tc, tc)
            hit = jnp.where(sb_ref[:, pl.ds(k0, tc)] >= cbt, jnp.ones((), BF16), jnp.zeros((), BF16))
            return hits + functools.reduce(jnp.add, [hit[:, b * LANES:(b + 1) * LANES] for b in range(ncb)])

        hits = lax.fori_loop(0, n_count, body, jnp.zeros((tq, LANES), BF16))
        return jnp.broadcast_to(jnp.sum(hits.astype(F32), axis=-1, keepdims=True), (tq, LANES))

    def hi_pattern(k16):
        return jnp.where(k16 >= 0, k16, k16 ^ jnp.int32(0x7FFF))

    def stage1(b, carry):
        t16, c_at = carry
        cand = t16 + lax.shift_left(jnp.int32(1), jnp.int32(15) - b)
        cnt = count_ge_bf16(_bf16_pattern(hi_pattern(cand)))
        ok = cnt >= ftopk
        return jnp.where(ok, cand, t16), jnp.where(ok, cnt, c_at)

    t16, c_at = lax.fori_loop(0, 16, stage1, (jnp.full((tq, LANES), -32768, I32), jnp.full((tq, LANES), 2.0 * L_MAX, F32)))
    few_valid = t16 <= -32768

    def unsettled(c):
        return jnp.max(jnp.where(few_valid | (c == ftopk), 0.0, 1.0)) > 0.0

    t16_t = jnp.concatenate([t16] * (ts // LANES), axis=1)

    def remap_tile(t, carry):
        k0 = pl.multiple_of(t * ts, ts)
        key = key_ref[:, pl.ds(k0, ts)]
        hi = lax.shift_right_arithmetic(key, 16)
        mid = (lax.shift_right_logical(key, 2) & jnp.int32(0x3FFF)) + jnp.int32(MID_BASE)
        pat = jnp.where(hi > t16_t, jnp.int32(MID_ABOVE), jnp.where(hi == t16_t, mid, 0))
        sb_ref[:, pl.ds(k0, ts)] = _bf16_pattern(pat).astype(BF16)
        return carry

    lax.fori_loop(0, n_cover, remap_tile, 0)

    def stage2(carry):
        b, u14, c = carry
        cand = u14 + lax.shift_left(jnp.int32(1), jnp.int32(13) - b)
        cnt = count_ge_bf16(_bf16_pattern(cand + jnp.int32(MID_BASE)))
        ok = cnt >= ftopk
        return b + 1, jnp.where(ok, cand, u14), jnp.where(ok, cnt, c)

    _, u14, c_at = lax.while_loop(lambda carry: jnp.logical_and(carry[0] < 14, unsettled(carry[2])), stage2,
                                  (jnp.int32(0), jnp.zeros((tq, LANES), I32), c_at))
    thr = lax.shift_left(t16, 16) + lax.shift_left(u14, 2)

    def count_ge_key(cand):
        def body(t, cnts):
            k0 = pl.multiple_of(t * tc, tc)
            cnts = list(cnts)
            for s in range(ncb):
                kk = key_ref[:, pl.ds(k0 + s * LANES, LANES)]
                cnts[s % 2] = cnts[s % 2] + jnp.where(kk >= cand, 1.0, 0.0)
            return tuple(cnts)

        z = jnp.zeros((tq, LANES), F32)
        c0, c1 = lax.fori_loop(0, n_count, body, (z, z))
        return jnp.broadcast_to(jnp.sum(c0 + c1, axis=-1, keepdims=True), (tq, LANES))

    def stage3(carry):
        k, thr, c = carry
        cand = thr + lax.shift_right_logical(jnp.int32(2), k)
        cnt = count_ge_key(cand)
        ok = cnt >= ftopk
        return k + 1, jnp.where(ok, cand, thr), jnp.where(ok, cnt, c)

    _, thr, c_at = lax.while_loop(lambda carry: jnp.logical_and(carry[0] < 2, unsettled(carry[2])), stage3,
                                  (jnp.int32(0), thr, c_at))
    thr = jnp.where(few_valid, INT_MIN + 1, jnp.maximum(thr, INT_MIN + 1))
    nab = ta // LANES
    thr_b = jnp.concatenate([thr] * nab, axis=1)

    rows = DSA_GQA * tq
    m_ref[...] = jnp.full(m_ref.shape, M_INIT, F32)
    acc_ref[...] = jnp.zeros(acc_ref.shape, F32)

    def attend(t, carry):
        k0 = pl.multiple_of(t * ta, ta)
        bias = jnp.where(key_ref[:, pl.ds(k0, ta)] >= thr_b, 0.0, NEG_BIG)
        bias4 = jnp.concatenate([bias] * DSA_GQA, axis=0)
        for n in range(DSA_KV_HEADS):
            kt = k_ref[pl.ds(k0, ta), (n // 2) * LANES:(n // 2 + 1) * LANES]
            s = _dot_nt(stacked(qx_ref, n * DSA_GQA, DSA_GQA), kt) + bias4
            blocks = [s[:, b * LANES:(b + 1) * LANES] for b in range(nab)]
            m_old = m_ref[n]
            m_new = jnp.maximum(m_old, jnp.max(functools.reduce(jnp.maximum, blocks), axis=-1, keepdims=True))
            p = jnp.concatenate([jnp.exp2(blk - m_new) for blk in blocks], axis=1).astype(BF16)
            acc_ref[n] = jnp.exp2(m_old - m_new) * acc_ref[n] + _dot(p, v1_ref[n, pl.ds(k0, ta), :])
            m_ref[n] = m_new
        return carry

    lax.fori_loop(0, n_att, attend, 0)
    outs = []
    for n in range(DSA_KV_HEADS):
        acc = acc_ref[n]
        o = acc * (1.0 / pltpu.roll(acc, DSA_HEADDIM, 1))
        outs.extend(o[e * tq:(e + 1) * tq, :] for e in range(DSA_GQA))
    for j in range(DSA_HEADS // 2):
        blk = jnp.where(lo_half, outs[2 * j], pltpu.roll(outs[2 * j + 1], DSA_HEADDIM, 1))
        cols = slice(j * LANES, (j + 1) * LANES)
        y_ref[:, cols] = (blk * _silu(g_ref[:, cols].astype(F32))).astype(BF16)


def _dsa_main(proj, qx, qix, kr, v1, kidup, wi, topk):
    L = proj.shape[0]
    tq = DSA_QBLOCK
    resident = lambda shape: pl.BlockSpec(shape, lambda i: (0,) * len(shape), pipeline_mode=pl.Buffered(1))
    return pl.pallas_call(
        functools.partial(_dsa_main_kernel, topk=topk),
        grid=(L // tq,),
        in_specs=[
            pl.BlockSpec((tq, DSA_HEADS * LANES), lambda i: (i, 0)),
            pl.BlockSpec((tq, IDX_HEADS * LANES), lambda i: (i, 0)),
            pl.BlockSpec((tq, LANES), lambda i: (i, 0)),
            pl.BlockSpec((tq, DSA_WIDTH), lambda i: (i, 2)),
            resident((L, LANES)),
            resident((L, 256)),
            resident((DSA_KV_HEADS, L, LANES)),
        ],
        out_specs=pl.BlockSpec((tq, DSA_WIDTH), lambda i: (i, 0)),
        out_shape=jax.ShapeDtypeStruct((L, DSA_WIDTH), BF16),
        scratch_shapes=[
            pltpu.VMEM((tq, L), I32),
            pltpu.VMEM((tq, L), BF16),
            pltpu.VMEM((IDX_HEADS, tq, LANES), F32),
            pltpu.VMEM((DSA_KV_HEADS, DSA_GQA * tq, LANES), F32),
            pltpu.VMEM((DSA_KV_HEADS, DSA_GQA * tq, LANES), F32),
        ],
        compiler_params=_cparams(("arbitrary",), VMEM_LIMIT_BYTES),
        name="dsa_index_attend",
    )(qx, qix, wi, proj, kidup, kr, v1)


def _dsa_layer(h, norm_g, w_in, idx_knorm, w_out, c, s1, s2):
    L = h.shape[0]
    q, k, v, g, qi, ki, wi = jnp.split(w_in, np.cumsum([1024, 256, 256, 1024, 1024, 64, 16])[:-1].tolist(), axis=1)
    w = jnp.concatenate([q, qi, g, k, v, ki, wi], axis=1)
    proj, kw = _norm_matmul(h, norm_g, w, DSA_IN_MAIN, tn=DSA_IN_MAIN // 2)
    qx, qix, kr, v1, kidup, wsc = _dsa_prep(proj, kw, c, s1, s2, idx_knorm)
    y = _dsa_main(proj, qx, qix, kr, v1, kidup, wsc, topk=min(TOPK_MAX, L // 4))
    return _out_proj(y, w_out.astype(BF16), h)


def kernel(x, positions, l0_norm, l0_w_in, l0_conv_w, l0_conv_b, l0_dt_bias, l0_a_log, l0_d_skip, l0_gnorm, l0_w_out, l1_norm, l1_w_in, l1_gnorm, l1_w_out, l2_norm, l2_w_in, l2_idx_knorm, l2_w_out, l3_norm, l3_w_in, l3_conv_w, l3_conv_b, l3_dt_bias, l3_a_log, l3_d_skip, l3_gnorm, l3_w_out, final_norm):
    b, L, d = x.shape
    outs = []
    for bi in range(b):
        h = x[bi]
        rc, rs, dc, ds1, ds2 = _rope_tables(positions[bi].astype(F32))
        h = _ssd_layer(h, l0_norm, l0_w_in, l0_conv_w, l0_conv_b, l0_dt_bias, l0_a_log, l0_d_skip, l0_gnorm, l0_w_out)
        h = _ret_layer(h, l1_norm, l1_w_in, l1_gnorm, l1_w_out, rc, rs)
        h = _dsa_layer(h, l2_norm, l2_w_in, l2_idx_knorm, l2_w_out, dc, ds1, ds2)
        h = _ssd_layer(h, l3_norm, l3_w_in, l3_conv_w, l3_conv_b, l3_dt_bias, l3_a_log, l3_d_skip, l3_gnorm, l3_w_out,
                       final_g=final_norm)
        outs.append(h)
    return jnp.stack(outs, axis=0)
```
